```python
import jax, jax.numpy as jnp
from jax import lax
import numpy as np

D_MODEL = 1024
BATCH = 32
SEQ = 256
DEPTH = 4
DEC_BATCH = 2
DEC_SEQ = 2048
PAST_LEN = 512

GRID_W = 64
N_MIXERS = 2
N_ATTN_LAYERS = (DEPTH + 1) // 2
N_CHUNK_LAYERS = DEPTH // 2
N_HEADS = 8
Q_RANK = 512
KV_RANK = 256
QK_NOPE = 128
QK_ROPE = 64
QK_DIM = QK_NOPE + QK_ROPE
V_DIM = 128
ATTN_WIDTH = N_HEADS * V_DIM
ATTN_IN = Q_RANK + KV_RANK + QK_ROPE + ATTN_WIDTH
AXIS_ROPE = QK_ROPE // 2
ROPE_THETA = 10000.0
Q_BLOCK = 128
CHUNK = 128
MLP_GROUPS = 8
MLP_WIDTH = 2 * D_MODEL
MLP_IN = 3 * MLP_WIDTH
EPS = 1e-6

kernel_name = 'hybrid_mla_chunkmlp_diffusion_step'


def rmsnorm(x, g):
    xf = x.astype(jnp.float32)
    y = xf * lax.rsqrt(jnp.mean(xf * xf, axis=-1, keepdims=True) + EPS)
    return y.astype(x.dtype) * g


def layernorm(x, g, b):
    xf = x.astype(jnp.float32)
    mu = jnp.mean(xf, axis=-1, keepdims=True)
    var = jnp.mean(jnp.square(xf - mu), axis=-1, keepdims=True)
    y = (xf - mu) * lax.rsqrt(var + EPS)
    return y.astype(x.dtype) * g + b


def modulation(cond, w_mod, b_mod):
    m = (jax.nn.silu(cond) @ w_mod + b_mod).reshape(-1, 1, 3 * D_MODEL)
    return m[..., :D_MODEL], m[..., D_MODEL:2 * D_MODEL], m[..., 2 * D_MODEL:]


def axial_rope_tables(n_tokens, dtype):
    rows = n_tokens // GRID_W
    row = jnp.repeat(jnp.arange(rows, dtype=jnp.float32), GRID_W)
    col = jnp.tile(jnp.arange(GRID_W, dtype=jnp.float32), rows)
    inv = 1.0 / (ROPE_THETA ** (jnp.arange(0, AXIS_ROPE, 2, dtype=jnp.float32) / AXIS_ROPE))
    ang = jnp.concatenate([row[:, None] * inv, col[:, None] * inv], axis=-1)
    return jnp.cos(ang).astype(dtype), jnp.sin(ang).astype(dtype)


def _rotate(xa, ca, sa):
    half = xa.shape[-1] // 2
    x1, x2 = xa[..., :half], xa[..., half:]
    return jnp.concatenate([x1 * ca - x2 * sa, x1 * sa + x2 * ca], axis=-1)


def apply_axial_rope(x, cos, sin):
    h = AXIS_ROPE // 2
    xr, xc = x[..., :AXIS_ROPE], x[..., AXIS_ROPE:]
    return jnp.concatenate([_rotate(xr, cos[..., :h], sin[..., :h]),
                            _rotate(xc, cos[..., h:], sin[..., h:])], axis=-1)


def block_attention(q, k, v):
    b, tq, h, dqk = q.shape
    nb = tq // Q_BLOCK
    scale = dqk ** -0.5
    qb = q.reshape(b, nb, Q_BLOCK, h, dqk).swapaxes(0, 1)

    def one_block(qi):
        s = jnp.einsum('bqhd,bkhd->bhqk', qi, k).astype(jnp.float32) * scale
        p = jax.nn.softmax(s, axis=-1).astype(v.dtype)
        return jnp.einsum('bhqk,bkhd->bqhd', p, v)

    out = lax.map(one_block, qb)
    return out.swapaxes(0, 1).reshape(b, tq, h, v.shape[-1])


def mla_project(h, w_in, q_norm_g, kv_norm_g, w_uq):
    b, t, _ = h.shape
    proj = h @ w_in
    c_q = proj[..., :Q_RANK]
    c_kv = proj[..., Q_RANK:Q_RANK + KV_RANK]
    k_pe = proj[..., Q_RANK + KV_RANK:Q_RANK + KV_RANK + QK_ROPE]
    z = proj[..., Q_RANK + KV_RANK + QK_ROPE:]
    q = (rmsnorm(c_q, q_norm_g) @ w_uq).reshape(b, t, N_HEADS, QK_DIM)
    return q, rmsnorm(c_kv, kv_norm_g), k_pe, z


def mla_keys_values(ckv, k_pe, w_ukv):
    b, t, _ = ckv.shape
    kv = (ckv @ w_ukv).reshape(b, t, N_HEADS, QK_NOPE + V_DIM)
    k_nope, v = kv[..., :QK_NOPE], kv[..., QK_NOPE:]
    k_rope = jnp.broadcast_to(k_pe[:, :, None, :], (b, t, N_HEADS, QK_ROPE))
    return jnp.concatenate([k_nope, k_rope], axis=-1), v


def gated_out(o, z, w_o):
    return (o * jax.nn.silu(z)) @ w_o


def chunk_mlp(h, w_in, v_g, v_b, w_s, b_s, w_o):
    b, t, _ = h.shape
    proj = h @ w_in
    uv = jax.nn.gelu(proj[..., :2 * MLP_WIDTH])
    z = proj[..., 2 * MLP_WIDTH:]
    u, v = uv[..., :MLP_WIDTH], uv[..., MLP_WIDTH:]
    v = layernorm(v, v_g, v_b).reshape(b, t // CHUNK, CHUNK, MLP_GROUPS, MLP_WIDTH // MLP_GROUPS)
    s = jnp.einsum('gpq,bcqgd->bcpgd', w_s, v) + b_s.T[:, :, None]
    return gated_out(u * s.reshape(b, t, MLP_WIDTH), z, w_o)


def setup_inputs(seed: int = 0) -> dict:
    key = jax.random.key(seed)
    ks = jax.random.split(key, 24)
    f32 = jnp.float32
    nrm = lambda k, shape, s: jax.random.normal(k, shape, f32) * s
    gain = lambda k, shape: 1.0 + 0.02 * jax.random.normal(k, shape, f32)
    D = D_MODEL
    return {
        'x_prompt': nrm(ks[0], (BATCH, SEQ, D), 1.0),
        'x_sample': nrm(ks[1], (DEC_BATCH, DEC_SEQ, D), 1.0),
        'cache_ckv': nrm(ks[2], (DEC_BATCH, N_ATTN_LAYERS, PAST_LEN, KV_RANK), 1.0),
        'cache_kpe': nrm(ks[3], (DEC_BATCH, N_ATTN_LAYERS, PAST_LEN, QK_ROPE), 1.0),
        'c': nrm(ks[4], (DEC_BATCH, D), 1.0),
        'c_ctx': nrm(ks[5], (D,), 1.0),
        'norm_g': gain(ks[6], (DEPTH, D)),
        'w_mod': nrm(ks[7], (DEPTH, D, 3 * D), 0.5 * D ** -0.5),
        'b_mod': nrm(ks[8], (DEPTH, 3 * D), 0.02),
        'attn_w_in': nrm(ks[9], (N_ATTN_LAYERS, D, ATTN_IN), D ** -0.5),
        'attn_q_norm_g': gain(ks[10], (N_ATTN_LAYERS, Q_RANK)),
        'attn_kv_norm_g': gain(ks[11], (N_ATTN_LAYERS, KV_RANK)),
        'attn_w_uq': nrm(ks[12], (N_ATTN_LAYERS, Q_RANK, N_HEADS * QK_DIM), Q_RANK ** -0.5),
        'attn_w_ukv': nrm(ks[13], (N_ATTN_LAYERS, KV_RANK, N_HEADS * (QK_NOPE + V_DIM)), KV_RANK ** -0.5),
        'attn_w_o': nrm(ks[14], (N_ATTN_LAYERS, ATTN_WIDTH, D), ATTN_WIDTH ** -0.5),
        'mlp_w_in': nrm(ks[15], (N_CHUNK_LAYERS, D, MLP_IN), D ** -0.5),
        'mlp_v_norm_g': gain(ks[16], (N_CHUNK_LAYERS, MLP_WIDTH)),
        'mlp_v_norm_b': nrm(ks[17], (N_CHUNK_LAYERS, MLP_WIDTH), 0.02),
        'mlp_w_s': nrm(ks[18], (N_CHUNK_LAYERS, MLP_GROUPS, CHUNK, CHUNK), CHUNK ** -0.5),
        'mlp_b_s': 1.0 + nrm(ks[19], (N_CHUNK_LAYERS, MLP_GROUPS, CHUNK), 0.02),
        'mlp_w_o': nrm(ks[20], (N_CHUNK_LAYERS, MLP_WIDTH, D), MLP_WIDTH ** -0.5),
        'final_norm_g': gain(ks[21], (D,)),
    }


def reference(x_prompt, x_sample, cache_ckv, cache_kpe, c, c_ctx, norm_g, w_mod, b_mod,
              attn_w_in, attn_q_norm_g, attn_kv_norm_g, attn_w_uq, attn_w_ukv, attn_w_o,
              mlp_w_in, mlp_v_norm_g, mlp_v_norm_b, mlp_w_s, mlp_b_s, mlp_w_o, final_norm_g):
    t_lat = x_sample.shape[1]
    cos, sin = axial_rope_tables(t_lat, x_sample.dtype)
    xc, xl = x_prompt, x_sample
    ckv_out, kpe_out = [], []
    for layer in range(DEPTH):
        sh_c, sc_c, g_c = modulation(c_ctx, w_mod[layer], b_mod[layer])
        sh_l, sc_l, g_l = modulation(c, w_mod[layer], b_mod[layer])
        hc = rmsnorm(xc, norm_g[layer]) * (1.0 + sc_c) + sh_c
        hl = rmsnorm(xl, norm_g[layer]) * (1.0 + sc_l) + sh_l
        if layer % N_MIXERS == 0:
            a = layer // N_MIXERS
            q_c, ckv_c, kpe_c, z_c = mla_project(hc, attn_w_in[a], attn_q_norm_g[a], attn_kv_norm_g[a], attn_w_uq[a])
            k_c, v_c = mla_keys_values(ckv_c, kpe_c, attn_w_ukv[a])
            o_c = block_attention(q_c, k_c, v_c).reshape(hc.shape[0], hc.shape[1], ATTN_WIDTH)
            mix_c = gated_out(o_c, z_c, attn_w_o[a])
            ckv_out.append(ckv_c)
            kpe_out.append(kpe_c)
            q_l, ckv_l, kpe_l, z_l = mla_project(hl, attn_w_in[a], attn_q_norm_g[a], attn_kv_norm_g[a], attn_w_uq[a])
            q_l = jnp.concatenate([q_l[..., :QK_NOPE],
                                   apply_axial_rope(q_l[..., QK_NOPE:], cos[:, None, :], sin[:, None, :])], axis=-1)
            kpe_l = apply_axial_rope(kpe_l, cos, sin)
            k_l, v_l = mla_keys_values(ckv_l, kpe_l, attn_w_ukv[a])
            k_p, v_p = mla_keys_values(cache_ckv[:, a], cache_kpe[:, a], attn_w_ukv[a])
            o_l = block_attention(q_l, jnp.concatenate([k_p, k_l], axis=1),
                                  jnp.concatenate([v_p, v_l], axis=1)).reshape(hl.shape[0], t_lat, ATTN_WIDTH)
            mix_l = gated_out(o_l, z_l, attn_w_o[a])
        else:
            m = layer // N_MIXERS
            mix_c = chunk_mlp(hc, mlp_w_in[m], mlp_v_norm_g[m], mlp_v_norm_b[m], mlp_w_s[m], mlp_b_s[m], mlp_w_o[m])
            mix_l = chunk_mlp(hl, mlp_w_in[m], mlp_v_norm_g[m], mlp_v_norm_b[m], mlp_w_s[m], mlp_b_s[m], mlp_w_o[m])
        xc = xc + g_c * mix_c
        xl = xl + g_l * mix_l
    y_prompt = rmsnorm(xc, final_norm_g)
    y_sample = rmsnorm(xl, final_norm_g)
    new_ckv = jnp.stack(ckv_out, axis=1)
    new_kpe = jnp.stack(kpe_out, axis=1)
    return (y_prompt, y_sample, new_ckv, new_kpe)
```

```python
import functools

import jax
import jax.numpy as jnp
from jax import lax
from jax.experimental import pallas as pl
from jax.experimental.pallas import tpu as pltpu

F32 = jnp.float32
BF16 = jnp.bfloat16

D = 1024
N_HEADS = 8
Q_RANK = 512
KV_RANK = 256
QK_NOPE = 128
QK_ROPE = 64
V_DIM = 128
ATTN_WIDTH = N_HEADS * V_DIM
AXIS_ROPE = QK_ROPE // 2
ROPE_THETA = 10000.0
GRID_W = 64
CHUNK = 128
MLP_GROUPS = 8
MLP_WIDTH = 2 * D
GROUP_W = MLP_WIDTH // MLP_GROUPS
EPS = 1e-6
SCALE = float((QK_NOPE + QK_ROPE) ** -0.5)

LANES = 128
ATTN_IN_PAD = Q_RANK + KV_RANK + LANES + ATTN_WIDTH
VMEM_LIMIT = 56 * 1024 * 1024

TOK_TILE = 256
Q_TILE = 256


def _params():
    return pltpu.CompilerParams(vmem_limit_bytes=VMEM_LIMIT)


def _dot(a, b):
    return jnp.dot(a, b, preferred_element_type=F32)


def _dot_nt(a, b):
    return lax.dot_general(a, b, (((1,), (1,)), ((), ())), preferred_element_type=F32)


def _rms(x):
    return x * lax.rsqrt(jnp.mean(x * x, axis=-1, keepdims=True) + EPS)


def _silu(x):
    return x * jax.nn.sigmoid(x)


def _mod_parts(mod_ref):
    m = mod_ref[0]
    return m[:, :D], m[:, D:2 * D], m[:, 2 * D:]


def _norm_mod(x, g, shift, scale):
    return _rms(x) * (g * (1.0 + scale)) + shift


def _mod_kernel(cond_ref, w_ref, b_ref, o_ref):
    a = _silu(cond_ref[...]).astype(BF16)
    o_ref[0] = _dot(a, w_ref[0].astype(BF16)) + b_ref[0]


def _modulation(cond8, w_mod, b_mod):
    depth = w_mod.shape[0]
    tn = 1024
    return pl.pallas_call(
        _mod_kernel,
        grid=(depth, 3 * D // tn),
        in_specs=[
            pl.BlockSpec((8, D), lambda l, j: (0, 0)),
            pl.BlockSpec((1, D, tn), lambda l, j: (l, 0, j)),
            pl.BlockSpec((1, 1, tn), lambda l, j: (l, 0, j)),
        ],
        out_specs=pl.BlockSpec((1, 8, tn), lambda l, j: (l, 0, j)),
        out_shape=jax.ShapeDtypeStruct((depth, 8, 3 * D), F32),
        compiler_params=_params(),
        name="modulation",
    )(cond8, w_mod, b_mod.reshape(depth, 1, 3 * D))


def _rope(x, cos, sin_a, sin_b):
    return x * cos + pltpu.roll(x, LANES - 16, 1) * sin_a + pltpu.roll(x, 16, 1) * sin_b


def _split_kpe(kpe2):
    lane = lax.broadcasted_iota(jnp.int32, kpe2.shape, 1)
    zero = jnp.zeros_like(kpe2)
    return jnp.where(lane < QK_ROPE, kpe2, zero), jnp.where(lane >= QK_ROPE, kpe2, zero)


def _mla_project(h, w_in_ref, gq_ref, gkv_ref, w_uq_ref, w_uk_ref, w_uvt_ref, rope):
    proj = _dot(h, w_in_ref[...])
    c_q = proj[:, :Q_RANK]
    c_kv = proj[:, Q_RANK:Q_RANK + KV_RANK]
    kpe2 = proj[:, Q_RANK + KV_RANK:Q_RANK + KV_RANK + LANES]
    z = proj[:, Q_RANK + KV_RANK + LANES:]
    q = _dot((_rms(c_q) * gq_ref[...]).astype(BF16), w_uq_ref[...]) * SCALE
    q_nope = q[:, :N_HEADS * QK_NOPE]
    q_rope = [q[:, N_HEADS * QK_NOPE + j * LANES:N_HEADS * QK_NOPE + (j + 1) * LANES]
              for j in range(N_HEADS // 2)]
    ckv = _rms(c_kv) * gkv_ref[...]
    ckv_b = ckv.astype(BF16)
    k_nope = _dot(ckv_b, w_uk_ref[...])
    v_t = _dot(w_uvt_ref[...], ckv.T.astype(BF16))
    kpe_rot = kpe2
    if rope is not None:
        cos, sin_a, sin_b = rope
        q_rope = [_rope(t, cos, sin_a, sin_b) for t in q_rope]
        kpe_rot = _rope(kpe2, cos, sin_a, sin_b)
    return q_nope, q_rope, k_nope, kpe2, kpe_rot, v_t, ckv, z


def _attend(q_h, segments):
    s = [_dot_nt(k, q_h) for k, _ in segments]
    m = s[0].max(axis=0, keepdims=True)
    for si in s[1:]:
        m = jnp.maximum(m, si.max(axis=0, keepdims=True))
    p = [jnp.exp(si - m) for si in s]
    denom = p[0].sum(axis=0, keepdims=True)
    for pi in p[1:]:
        denom = denom + pi.sum(axis=0, keepdims=True)
    o_t = _dot(segments[0][1], p[0].astype(BF16))
    for (_, v_t), pi in zip(segments[1:], p[1:]):
        o_t = o_t + _dot(v_t, pi.astype(BF16))
    return o_t / denom


def _gated_residual(x, o, z, gate, w_o_ref):
    mix = _dot((o * _silu(z)).astype(BF16), w_o_ref[...])
    return x + gate * mix


def _ctx_attn_kernel(x_ref, mod_ref, g_ref, w_in_ref, gq_ref, gkv_ref, w_uq_ref, w_uk_ref,
                     w_uvt_ref, w_o_ref, y_ref, ckv_ref, kpe_ref, ot_ref):
    x = x_ref[0]
    shift, scale, gate = _mod_parts(mod_ref)
    h = _norm_mod(x, g_ref[...], shift, scale).astype(BF16)
    q_nope, q_rope, k_nope, kpe2, kpe_rot, v_t, ckv, z = _mla_project(
        h, w_in_ref, gq_ref, gkv_ref, w_uq_ref, w_uk_ref, w_uvt_ref, None)
    ckv_ref[0] = ckv
    kpe_ref[0] = kpe2[:, :QK_ROPE]
    kpe_tiles = [t.astype(BF16) for t in _split_kpe(kpe_rot)]
    q_nope = q_nope.astype(BF16)
    q_rope = [t.astype(BF16) for t in q_rope]
    k_nope = k_nope.astype(BF16)
    v_t = v_t.astype(BF16)
    for hd in range(N_HEADS):
        sl = slice(hd * LANES, (hd + 1) * LANES)
        q_h = jnp.concatenate([q_nope[:, sl], q_rope[hd // 2]], axis=1)
        k_h = jnp.concatenate([k_nope[:, sl], kpe_tiles[hd % 2]], axis=1)
        ot_ref[sl, :] = _attend(q_h, [(k_h, v_t[sl, :])])
    y_ref[0] = _gated_residual(x, ot_ref[...].T, z, gate, w_o_ref)


def _full(shape):
    n = len(shape)
    return pl.BlockSpec(shape, lambda *_: (0,) * n)


def _ctx_attn_layer(x, mod, g, w):
    b, t, _ = x.shape
    return pl.pallas_call(
        _ctx_attn_kernel,
        grid=(b,),
        in_specs=[
            pl.BlockSpec((1, t, D), lambda i: (i, 0, 0)),
            pl.BlockSpec((1, 1, 3 * D), lambda i: (0, 0, 0)),
            _full((1, D)),
            _full((D, ATTN_IN_PAD)),
            _full((1, Q_RANK)),
            _full((1, KV_RANK)),
            _full((Q_RANK, N_HEADS * (QK_NOPE + QK_ROPE))),
            _full((KV_RANK, N_HEADS * QK_NOPE)),
            _full((N_HEADS * V_DIM, KV_RANK)),
            _full((ATTN_WIDTH, D)),
        ],
        out_specs=[
            pl.BlockSpec((1, t, D), lambda i: (i, 0, 0)),
            pl.BlockSpec((1, t, KV_RANK), lambda i: (i, 0, 0)),
            pl.BlockSpec((1, t, QK_ROPE), lambda i: (i, 0, 0)),
        ],
        out_shape=[
            jax.ShapeDtypeStruct((b, t, D), F32),
            jax.ShapeDtypeStruct((b, t, KV_RANK), F32),
            jax.ShapeDtypeStruct((b, t, QK_ROPE), F32),
        ],
        scratch_shapes=[pltpu.VMEM((ATTN_WIDTH, t), F32)],
        compiler_params=_params(),
        name="ctx_attn",
    )(x, mod, g, w["w_in"], w["gq"], w["gkv"], w["w_uq"], w["w_uk"], w["w_uvt"], w["w_o"])


def _cache_kv_kernel(ckv_ref, kpe2_ref, w_uk_ref, w_uvt_ref, kn_ref, kr_ref, vt_ref):
    ckv = ckv_ref[0, 0]
    kn_ref[0, 0] = _dot(ckv.astype(BF16), w_uk_ref[0]).astype(BF16)
    vt_ref[0, 0] = _dot(w_uvt_ref[0], ckv.T.astype(BF16)).astype(BF16)
    even, odd = _split_kpe(kpe2_ref[0, 0])
    kr_ref[0, 0] = jnp.concatenate([even, odd], axis=1).astype(BF16)


def _cache_kv(cache_ckv, cache_kpe2, w_uk, w_uvt):
    b, n_attn, past, _ = cache_ckv.shape
    hk = N_HEADS * QK_NOPE
    return pl.pallas_call(
        _cache_kv_kernel,
        grid=(n_attn, b),
        in_specs=[
            pl.BlockSpec((1, 1, past, KV_RANK), lambda a, i: (i, a, 0, 0)),
            pl.BlockSpec((1, 1, past, LANES), lambda a, i: (i, a, 0, 0)),
            pl.BlockSpec((1, KV_RANK, hk), lambda a, i: (a, 0, 0)),
            pl.BlockSpec((1, ATTN_WIDTH, KV_RANK), lambda a, i: (a, 0, 0)),
        ],
        out_specs=[
            pl.BlockSpec((1, 1, past, hk), lambda a, i: (a, i, 0, 0)),
            pl.BlockSpec((1, 1, past, 2 * LANES), lambda a, i: (a, i, 0, 0)),
            pl.BlockSpec((1, 1, ATTN_WIDTH, past), lambda a, i: (a, i, 0, 0)),
        ],
        out_shape=[
            jax.ShapeDtypeStruct((n_attn, b, past, hk), BF16),
            jax.ShapeDtypeStruct((n_attn, b, past, 2 * LANES), BF16),
            jax.ShapeDtypeStruct((n_attn, b, ATTN_WIDTH, past), BF16),
        ],
        compiler_params=_params(),
        name="cache_kv",
    )(cache_ckv, cache_kpe2, w_uk, w_uvt)


def _lat_proj_kernel(x_ref, mod_ref, g_ref, cos_ref, sa_ref, sb_ref, w_in_ref, gq_ref, gkv_ref,
                     w_uq_ref, w_uk_ref, w_uvt_ref, qn_ref, qr_ref, kn_ref, kr_ref, vt_ref, z_ref):
    shift, scale, _ = _mod_parts(mod_ref)
    h = _norm_mod(x_ref[0], g_ref[...], shift, scale).astype(BF16)
    rope = (cos_ref[...], sa_ref[...], sb_ref[...])
    q_nope, q_rope, k_nope, _, kpe_rot, v_t, _, z = _mla_project(
        h, w_in_ref, gq_ref, gkv_ref, w_uq_ref, w_uk_ref, w_uvt_ref, rope)
    qn_ref[0] = q_nope.astype(BF16)
    qr_ref[0] = jnp.concatenate(q_rope, axis=1).astype(BF16)
    kn_ref[0] = k_nope.astype(BF16)
    even, odd = _split_kpe(kpe_rot)
    kr_ref[0] = jnp.concatenate([even, odd], axis=1).astype(BF16)
    vt_ref[0] = v_t.astype(BF16)
    z_ref[0] = z


def _lat_proj(x, mod, g, rope, w):
    b, t, _ = x.shape
    tm = TOK_TILE
    hk = N_HEADS * QK_NOPE
    hr = N_HEADS * QK_ROPE
    row = lambda i, j: (i, j, 0)
    tab = pl.BlockSpec((tm, LANES), lambda i, j: (j, 0))
    return pl.pallas_call(
        _lat_proj_kernel,
        grid=(b, t // tm),
        in_specs=[
            pl.BlockSpec((1, tm, D), row),
            pl.BlockSpec((1, 1, 3 * D), lambda i, j: (1 + i, 0, 0)),
            _full((1, D)),
            tab, tab, tab,
            _full((D, ATTN_IN_PAD)),
            _full((1, Q_RANK)),
            _full((1, KV_RANK)),
            _full((Q_RANK, hk + hr)),
            _full((KV_RANK, hk)),
            _full((ATTN_WIDTH, KV_RANK)),
        ],
        out_specs=[
            pl.BlockSpec((1, tm, hk), row),
            pl.BlockSpec((1, tm, hr), row),
            pl.BlockSpec((1, tm, hk), row),
            pl.BlockSpec((1, tm, 2 * LANES), row),
            pl.BlockSpec((1, ATTN_WIDTH, tm), lambda i, j: (i, 0, j)),
            pl.BlockSpec((1, tm, ATTN_WIDTH), row),
        ],
        out_shape=[
            jax.ShapeDtypeStruct((b, t, hk), BF16),
            jax.ShapeDtypeStruct((b, t, hr), BF16),
            jax.ShapeDtypeStruct((b, t, hk), BF16),
            jax.ShapeDtypeStruct((b, t, 2 * LANES), BF16),
            jax.ShapeDtypeStruct((b, ATTN_WIDTH, t), BF16),
            jax.ShapeDtypeStruct((b, t, ATTN_WIDTH), F32),
        ],
        compiler_params=_params(),
        name="lat_proj",
    )(x, mod, g, *rope, w["w_in"], w["gq"], w["gkv"], w["w_uq"], w["w_uk"], w["w_uvt"])


def _lat_attn_kernel(x_ref, mod_ref, qn_ref, qr_ref, z_ref, knc_ref, krc_ref, vtc_ref,
                     kn_ref, kr_ref, vt_ref, w_o_ref, y_ref, ot_ref):
    _, _, gate = _mod_parts(mod_ref)
    for hd in range(N_HEADS):
        sl = slice(hd * LANES, (hd + 1) * LANES)
        rl = slice((hd % 2) * LANES, (hd % 2 + 1) * LANES)
        ql = slice((hd // 2) * LANES, (hd // 2 + 1) * LANES)
        q_h = jnp.concatenate([qn_ref[0, :, sl], qr_ref[0, :, ql]], axis=1)
        k_c = jnp.concatenate([knc_ref[0, 0, :, sl], krc_ref[0, 0, :, rl]], axis=1)
        k_l = jnp.concatenate([kn_ref[0, :, sl], kr_ref[0, :, rl]], axis=1)
        ot_ref[sl, :] = _attend(q_h, [(k_c, vtc_ref[0, 0, sl, :]), (k_l, vt_ref[0, sl, :])])
    y_ref[0] = _gated_residual(x_ref[0], ot_ref[...].T, z_ref[0], gate, w_o_ref)


def _lat_attn(x, mod, a, qn, qr, z, knc, krc, vtc, kn, kr, vt, w_o):
    b, t, _ = x.shape
    tq = Q_TILE
    past = knc.shape[2]
    hk = N_HEADS * QK_NOPE
    hr = N_HEADS * QK_ROPE
    row = lambda i, j: (i, j, 0)
    per_b = lambda i, j: (i, 0, 0)
    cache = lambda i, j: (a, i, 0, 0)
    return pl.pallas_call(
        _lat_attn_kernel,
        grid=(b, t // tq),
        in_specs=[
            pl.BlockSpec((1, tq, D), row),
            pl.BlockSpec((1, 1, 3 * D), lambda i, j: (1 + i, 0, 0)),
            pl.BlockSpec((1, tq, hk), row),
            pl.BlockSpec((1, tq, hr), row),
            pl.BlockSpec((1, tq, ATTN_WIDTH), row),
            pl.BlockSpec((1, 1, past, hk), cache),
            pl.BlockSpec((1, 1, past, 2 * LANES), cache),
            pl.BlockSpec((1, 1, ATTN_WIDTH, past), cache),
            pl.BlockSpec((1, t, hk), per_b),
            pl.BlockSpec((1, t, 2 * LANES), per_b),
            pl.BlockSpec((1, ATTN_WIDTH, t), per_b),
            _full((ATTN_WIDTH, D)),
        ],
        out_specs=pl.BlockSpec((1, tq, D), row),
        out_shape=jax.ShapeDtypeStruct((b, t, D), F32),
        scratch_shapes=[pltpu.VMEM((ATTN_WIDTH, tq), F32)],
        compiler_params=_params(),
        name="lat_attn",
    )(x, mod, qn, qr, z, knc, krc, vtc, kn, kr, vt, w_o)


def _mlp_kernel(x_ref, mod_ref, g_ref, w_in_ref, vg_ref, vb_ref, w_s_ref, b_s_ref, w_o_ref,
                fg_ref, y_ref, v_ref, gated_ref, *, final_norm):
    x = x_ref[0]
    tm = x.shape[0]
    shift, scale, gate = _mod_parts(mod_ref)
    h = _norm_mod(x, g_ref[...], shift, scale).astype(BF16)

    for g in range(MLP_GROUPS):
        cols = slice(MLP_WIDTH + g * GROUP_W, MLP_WIDTH + (g + 1) * GROUP_W)
        v_ref[:, g * GROUP_W:(g + 1) * GROUP_W] = jax.nn.gelu(_dot(h, w_in_ref[:, cols]))
    v = v_ref[...]
    mu = jnp.mean(v, axis=-1, keepdims=True)
    vc = v - mu
    var = jnp.mean(vc * vc, axis=-1, keepdims=True)
    v_ref[...] = vc * lax.rsqrt(var + EPS) * vg_ref[...] + vb_ref[...]

    for g in range(MLP_GROUPS):
        ucols = slice(g * GROUP_W, (g + 1) * GROUP_W)
        zcols = slice(2 * MLP_WIDTH + g * GROUP_W, 2 * MLP_WIDTH + (g + 1) * GROUP_W)
        u = jax.nn.gelu(_dot(h, w_in_ref[:, ucols]))
        z = _dot(h, w_in_ref[:, zcols])
        bias = b_s_ref[:, g:g + 1]
        s = jnp.concatenate(
            [_dot(w_s_ref[g], v_ref[c * CHUNK:(c + 1) * CHUNK, ucols].astype(BF16)) + bias
             for c in range(tm // CHUNK)], axis=0)
        gated_ref[:, ucols] = (u * s * _silu(z)).astype(BF16)

    y = x + gate * _dot(gated_ref[...], w_o_ref[...])
    if final_norm:
        y = _rms(y) * fg_ref[...]
    y_ref[0] = y


def _mlp_layer(x, mod, mod_row0, g, w, final_g, final_norm):
    b, t, _ = x.shape
    tm = TOK_TILE
    row = lambda i, j: (i, j, 0)
    return pl.pallas_call(
        functools.partial(_mlp_kernel, final_norm=final_norm),
        grid=(b, t // tm),
        in_specs=[
            pl.BlockSpec((1, tm, D), row),
            pl.BlockSpec((1, 1, 3 * D), lambda i, j: (mod_row0 + i, 0, 0)),
            _full((1, D)),
            _full((D, 3 * MLP_WIDTH)),
            _full((1, MLP_WIDTH)),
            _full((1, MLP_WIDTH)),
            _full((MLP_GROUPS, CHUNK, CHUNK)),
            _full((CHUNK, MLP_GROUPS)),
            _full((MLP_WIDTH, D)),
            _full((1, D)),
        ],
        out_specs=pl.BlockSpec((1, tm, D), row),
        out_shape=jax.ShapeDtypeStruct((b, t, D), F32),
        scratch_shapes=[pltpu.VMEM((tm, MLP_WIDTH), F32), pltpu.VMEM((tm, MLP_WIDTH), BF16)],
        compiler_params=_params(),
        name="mlp",
    )(x, mod, g, w["w_in"], w["vg"], w["vb"], w["w_s"], w["b_s_t"], w["w_o"], final_g)


def _rope_tables(n_tokens):
    rows = n_tokens // GRID_W
    row = jnp.repeat(jnp.arange(rows, dtype=F32), GRID_W)
    col = jnp.tile(jnp.arange(GRID_W, dtype=F32), rows)
    inv = 1.0 / (ROPE_THETA ** (jnp.arange(0, AXIS_ROPE, 2, dtype=F32) / AXIS_ROPE))
    ang_r, ang_c = row[:, None] * inv, col[:, None] * inv
    cr, sr, cc, sc = jnp.cos(ang_r), jnp.sin(ang_r), jnp.cos(ang_c), jnp.sin(ang_c)
    zero = jnp.zeros_like(sr)
    cos = jnp.concatenate([cr, cr, cc, cc], axis=1)
    sin_a = jnp.concatenate([-sr, zero, -sc, zero], axis=1)
    sin_b = jnp.concatenate([zero, sr, zero, sc], axis=1)
    return tuple(jnp.tile(t, (1, LANES // QK_ROPE)) for t in (cos, sin_a, sin_b))


def _attn_weights(w_in, gq, gkv, w_uq, w_ukv, w_o):
    kpe0 = Q_RANK + KV_RANK
    w_in_p = jnp.concatenate(
        [w_in[:, :kpe0 + QK_ROPE], w_in[:, kpe0:kpe0 + QK_ROPE], w_in[:, kpe0 + QK_ROPE:]], axis=1)
    uq = w_uq.reshape(Q_RANK, N_HEADS, QK_NOPE + QK_ROPE)
    w_uq_p = jnp.concatenate([uq[:, :, :QK_NOPE].reshape(Q_RANK, -1),
                              uq[:, :, QK_NOPE:].reshape(Q_RANK, -1)], axis=1)
    ukv = w_ukv.reshape(KV_RANK, N_HEADS, QK_NOPE + V_DIM)
    w_uk = ukv[:, :, :QK_NOPE].reshape(KV_RANK, -1)
    w_uvt = ukv[:, :, QK_NOPE:].reshape(KV_RANK, -1).T
    return dict(w_in=w_in_p.astype(BF16), gq=gq.reshape(1, -1), gkv=gkv.reshape(1, -1),
                w_uq=w_uq_p.astype(BF16), w_uk=w_uk.astype(BF16), w_uvt=w_uvt.astype(BF16),
                w_o=w_o.astype(BF16))


def _mlp_weights(w_in, vg, vb, w_s, b_s, w_o):
    return dict(w_in=w_in.astype(BF16), vg=vg.reshape(1, -1), vb=vb.reshape(1, -1),
                w_s=w_s.astype(BF16), b_s_t=b_s.T, w_o=w_o.astype(BF16))


def kernel(x_prompt, x_sample, cache_ckv, cache_kpe, c, c_ctx, norm_g, w_mod, b_mod, attn_w_in, attn_q_norm_g, attn_kv_norm_g, attn_w_uq, attn_w_ukv, attn_w_o, mlp_w_in, mlp_v_norm_g, mlp_v_norm_b, mlp_w_s, mlp_b_s, mlp_w_o, final_norm_g):
    depth = norm_g.shape[0]
    n_attn = attn_w_in.shape[0]
    bc, tc, _ = x_prompt.shape
    bl, tl, _ = x_sample.shape

    cond8 = jnp.concatenate([c_ctx[None, :], c, jnp.zeros((8 - 1 - bl, D), F32)], axis=0)
    mods = _modulation(cond8, w_mod, b_mod)
    rope = _rope_tables(tl)
    final_g = final_norm_g.reshape(1, D)

    aw = [_attn_weights(attn_w_in[a], attn_q_norm_g[a], attn_kv_norm_g[a], attn_w_uq[a],
                        attn_w_ukv[a], attn_w_o[a]) for a in range(n_attn)]
    knc, krc, vtc = _cache_kv(cache_ckv, jnp.tile(cache_kpe, (1, 1, 1, LANES // QK_ROPE)),
                              jnp.stack([w["w_uk"] for w in aw]), jnp.stack([w["w_uvt"] for w in aw]))

    xc, xl = x_prompt, x_sample
    ckv_out, kpe_out = [], []
    for layer in range(depth):
        mod = mods[layer].reshape(8, 1, 3 * D)
        g = norm_g[layer].reshape(1, D)
        if layer % 2 == 0:
            a = layer // 2
            xc, ckv_c, kpe_c = _ctx_attn_layer(xc, mod, g, aw[a])
            ckv_out.append(ckv_c)
            kpe_out.append(kpe_c)
            qn, qr, kn, kr, vt, z = _lat_proj(xl, mod, g, rope, aw[a])
            xl = _lat_attn(xl, mod, a, qn, qr, z, knc, krc, vtc, kn, kr, vt, aw[a]["w_o"])
        else:
            m = layer // 2
            mw = _mlp_weights(mlp_w_in[m], mlp_v_norm_g[m], mlp_v_norm_b[m], mlp_w_s[m],
                              mlp_b_s[m], mlp_w_o[m])
            last = layer == depth - 1
            xc = _mlp_layer(xc.reshape(1, bc * tc, D), mod, 0, g, mw, final_g, last).reshape(bc, tc, D)
            xl = _mlp_layer(xl, mod, 1, g, mw, final_g, last)
    if depth % 2 == 1:
        raise NotImplementedError("final RMSNorm is fused into a trailing gMLP layer")
    return xc, xl, jnp.stack(ckv_out, axis=1), jnp.stack(kpe_out, axis=1)
```

```python
import functools
import math

import jax
import jax.numpy as jnp
import numpy as np
from jax import lax
from jax.experimental import pallas as pl
from jax.experimental.pallas import tpu as pltpu

F32 = jnp.float32
BF16 = jnp.bfloat16

D = 1024
N_HEADS = 8
Q_RANK = 512
KV_RANK = 256
QK_NOPE = 128
QK_ROPE = 64
V_DIM = 128
ATTN_WIDTH = N_HEADS * V_DIM
AXIS_ROPE = QK_ROPE // 2
ROPE_THETA = 10000.0
GRID_W = 64
CHUNK = 128
MLP_GROUPS = 8
MLP_WIDTH = 2 * D
GROUP_W = MLP_WIDTH // MLP_GROUPS
EPS = 1e-6
QK_SCALE = float((QK_NOPE + QK_ROPE) ** -0.5 * math.log2(math.e))

LANES = 128
ATTN_IN_PAD = Q_RANK + KV_RANK + LANES + ATTN_WIDTH
VMEM_LIMIT = 56 * 1024 * 1024

TOK_TILE = 512
Q_TILE = 256
CTX_ROWS = 2
QK_LEAD = 3
V_LEAD = 2
SPATIAL_LEAD = 2
KEY_CHUNK = 512


def _params():
    return pltpu.CompilerParams(vmem_limit_bytes=VMEM_LIMIT)


def _dot(a, b):
    return jnp.dot(a, b, preferred_element_type=F32)


def _dot_nt(a, b):
    return lax.dot_general(a, b, (((1,), (1,)), ((), ())), preferred_element_type=F32)


def _rms(x):
    return x * lax.rsqrt(jnp.mean(x * x, axis=-1, keepdims=True) + EPS)


def _silu(x):
    return x * jax.nn.sigmoid(x)


def _mod_parts(mod_ref):
    m = mod_ref[0]
    return m[:, :D], m[:, D:2 * D], m[:, 2 * D:]


def _norm_mod(x, g, shift, scale):
    return _rms(x) * (g * (1.0 + scale)) + shift


def _full(shape):
    n = len(shape)
    return pl.BlockSpec(shape, lambda *_: (0,) * n)


def _layer(shape, l, single_buffer=False):
    n = len(shape)
    mode = dict(pipeline_mode=pl.Buffered(1)) if single_buffer else {}
    return pl.BlockSpec((None,) + tuple(shape), lambda *_: (l,) + (0,) * n, **mode)


def _mod_kernel(cond_ref, w_ref, b_ref, o_ref):
    a = _silu(cond_ref[...]).astype(BF16)
    o_ref[0] = _dot(a, w_ref[0].astype(BF16)) + b_ref[0]


def _modulation(cond8, w_mod, b_mod):
    depth = w_mod.shape[0]
    tn = 1024
    return pl.pallas_call(
        _mod_kernel,
        grid=(depth, 3 * D // tn),
        in_specs=[
            pl.BlockSpec((8, D), lambda l, j: (0, 0)),
            pl.BlockSpec((1, D, tn), lambda l, j: (l, 0, j)),
            pl.BlockSpec((1, 1, tn), lambda l, j: (l, 0, j)),
        ],
        out_specs=pl.BlockSpec((1, 8, tn), lambda l, j: (l, 0, j)),
        out_shape=jax.ShapeDtypeStruct((depth, 8, 3 * D), F32),
        compiler_params=_params(),
        name="modulation",
    )(cond8, w_mod, b_mod.reshape(depth, 1, 3 * D))


def _rope(x, cos, sin_a, sin_b):
    return x * cos + pltpu.roll(x, LANES - 16, 1) * sin_a + pltpu.roll(x, 16, 1) * sin_b


def _kpe_tile(kpe2):
    lane = lax.broadcasted_iota(jnp.int32, kpe2.shape, 1)
    return jnp.where(lane < QK_ROPE, kpe2, jnp.zeros_like(kpe2))


def _q_heads(q, rope):
    lane = lax.broadcasted_iota(jnp.int32, (q.shape[0], LANES), 1)
    heads = []
    for pair in range(N_HEADS // 2):
        t0, t1, t2 = (q[:, (3 * pair + j) * LANES:(3 * pair + j + 1) * LANES] for j in range(3))
        nope_odd = jnp.where(lane < QK_ROPE, pltpu.roll(t1, QK_ROPE, 1), pltpu.roll(t2, QK_ROPE, 1))
        rot = jnp.where(lane < QK_ROPE, t1, t2)
        if rope is not None:
            rot = _rope(rot, *rope)
        heads += [(t0, rot), (nope_odd, pltpu.roll(rot, QK_ROPE, 1))]
    return heads


def _mla_project(h, w_in_ref, gq_ref, gkv_ref, w_uq_ref, w_uk_ref, w_uvt_ref, rope):
    proj = _dot_nt(h, w_in_ref[...])
    c_q = proj[:, :Q_RANK]
    c_kv = proj[:, Q_RANK:Q_RANK + KV_RANK]
    kpe2 = proj[:, Q_RANK + KV_RANK:Q_RANK + KV_RANK + LANES]
    z = proj[:, Q_RANK + KV_RANK + LANES:]
    q = _dot((_rms(c_q) * gq_ref[...]).astype(BF16), w_uq_ref[...]) * QK_SCALE
    ckv = _rms(c_kv) * gkv_ref[...]
    k_nope = _dot(ckv.astype(BF16), w_uk_ref[...])
    v_t = _dot(w_uvt_ref[...], ckv.T.astype(BF16))
    kpe_rot = kpe2 if rope is None else _rope(kpe2, *rope)
    return _q_heads(q, rope), k_nope, kpe2, _kpe_tile(kpe_rot), v_t, ckv, z


def _gated_residual(x, o, z, gate, w_o_ref):
    mix = _dot((o * _silu(z)).astype(BF16), w_o_ref[...])
    return x + gate * mix


def _ctx_attn_kernel(*refs, n_prev, n_cast):
    (x_ref, mod_ref, g_ref, w_in_ref, gq_ref, gkv_ref, w_uq_ref, w_uk_ref, w_uvt_ref,
     w_o_ref) = refs[:10]
    refs = refs[10:]
    prev, refs = refs[:2 * bool(n_prev)], refs[2 * bool(n_prev):]
    cast_in, refs = refs[:n_cast], refs[n_cast:]
    (y_ref, ckv_ref, kpet_ref), refs = refs[:3], refs[3:]
    cast_out, (s_ref, p_ref) = refs[:n_cast], refs[n_cast:]
    for src, dst in zip(cast_in, cast_out):
        dst[...] = src[...].astype(BF16)
    shift, scale, gate = _mod_parts(mod_ref)
    if n_prev:
        ckv_ref[:, :n_prev] = prev[0][...]
        kpet_ref[:, :n_prev] = prev[1][...]
    for r in range(x_ref.shape[0]):
        x = x_ref[r]
        h = _norm_mod(x, g_ref[...], shift, scale).astype(BF16)
        q_heads, k_nope, kpe2, kpe_tile, v_t, ckv, z = _mla_project(
            h, w_in_ref, gq_ref, gkv_ref, w_uq_ref, w_uk_ref, w_uvt_ref, None)
        ckv_ref[r, n_prev] = ckv
        kpet_ref[r, n_prev] = kpe2.T[:QK_ROPE]
        kpe_tile = kpe_tile.astype(BF16)
        k_nope = k_nope.astype(BF16)
        v_t = v_t.astype(BF16)
        for hd in range(N_HEADS):
            q_h = jnp.concatenate([t.astype(BF16) for t in q_heads[hd]], axis=1)
            k_h = jnp.concatenate([k_nope[:, hd * LANES:(hd + 1) * LANES], kpe_tile], axis=1)
            s_ref[r, hd] = _dot_nt(k_h, q_h)
        inv = []
        for hd in range(N_HEADS):
            s = s_ref[r, hd]
            p = jnp.exp2(s - s.max(axis=0, keepdims=True))
            inv.append(1.0 / p.sum(axis=0, keepdims=True))
            p_ref[r, hd] = p.astype(BF16)
        o_t = [_dot(v_t[hd * LANES:(hd + 1) * LANES, :], p_ref[r, hd]) * inv[hd]
               for hd in range(N_HEADS)]
        o = jnp.concatenate([t.T for t in o_t], axis=1)
        y_ref[r] = _gated_residual(x, o, z, gate, w_o_ref)


def _ctx_attn_layer(x, mod, mod0, layer, a, norm_g, w, prev, riders=()):
    b, t, _ = x.shape
    rows = CTX_ROWS
    steps = b // rows
    blk = lambda i: (i, 0, 0, 0)
    chunk = lambda p: pl.BlockSpec((p.shape[0], p.shape[1] // steps, p.shape[2]), lambda i: (0, i, 0))
    return pl.pallas_call(
        functools.partial(_ctx_attn_kernel, n_prev=a, n_cast=len(riders)),
        grid=(steps,),
        in_specs=[
            pl.BlockSpec((rows, t, D), lambda i: (i, 0, 0)),
            pl.BlockSpec((1, 1, 3 * D), lambda i: (mod0, 0, 0)),
            _layer((1, D), layer),
            _layer((ATTN_IN_PAD, D), a),
            _layer((1, Q_RANK), a),
            _layer((1, KV_RANK), a),
            _layer((Q_RANK, N_HEADS * (QK_NOPE + QK_ROPE)), a),
            _layer((KV_RANK, N_HEADS * QK_NOPE), a),
            _layer((N_HEADS * V_DIM, KV_RANK), a),
            _layer((ATTN_WIDTH, D), a),
        ] + ([pl.BlockSpec((rows, a, t, KV_RANK), blk), pl.BlockSpec((rows, a, QK_ROPE, t), blk)]
             if a else []) + [chunk(p) for p in riders],
        out_specs=[
            pl.BlockSpec((rows, t, D), lambda i: (i, 0, 0)),
            pl.BlockSpec((rows, a + 1, t, KV_RANK), blk),
            pl.BlockSpec((rows, a + 1, QK_ROPE, t), blk),
        ] + [chunk(p) for p in riders],
        out_shape=[
            jax.ShapeDtypeStruct((b, t, D), F32),
            jax.ShapeDtypeStruct((b, a + 1, t, KV_RANK), F32),
            jax.ShapeDtypeStruct((b, a + 1, QK_ROPE, t), F32),
        ] + [jax.ShapeDtypeStruct(p.shape, BF16) for p in riders],
        scratch_shapes=[pltpu.VMEM((rows, N_HEADS, t, t), F32), pltpu.VMEM((rows, N_HEADS, t, t), BF16)],
        compiler_params=_params(),
        name="ctx_attn",
    )(x, mod, norm_g, w["w_in"], w["gq"], w["gkv"], w["w_uq"], w["w_uk"], w["w_uvt"], w["w_o"], *prev,
      *riders)


def _store_heads(ref, lead, tiles):
    for hd, (nope, rot) in enumerate(tiles):
        ref[lead + (hd, slice(None), slice(0, LANES))] = nope.astype(BF16)
        ref[lead + (hd, slice(None), slice(LANES, 2 * LANES))] = rot.astype(BF16)


def _key_tiles(k_nope, kpe_tile):
    return [(k_nope[:, hd * LANES:(hd + 1) * LANES], kpe_tile) for hd in range(N_HEADS)]


def _cache_kv_kernel(ckv_ref, kpet_ref, w_uk_ref, w_uvt_ref, k_ref, vt_ref):
    ckv = ckv_ref[0, 0]
    k_nope = _dot(ckv.astype(BF16), w_uk_ref[0])
    kpe_t = kpet_ref[0, 0]
    kpe2 = jnp.concatenate([kpe_t] * (LANES // QK_ROPE), axis=0).T
    _store_heads(k_ref, (0, 0), _key_tiles(k_nope, _kpe_tile(kpe2)))
    v_t = _dot(w_uvt_ref[0], ckv.T.astype(BF16)).astype(BF16)
    for hd in range(N_HEADS):
        vt_ref[0, 0, hd] = v_t[hd * V_DIM:(hd + 1) * V_DIM, :]


def _cache_kv(cache_ckv, cache_kpe_t, w_uk, w_uvt):
    b, n_attn, past, _ = cache_ckv.shape
    hk = N_HEADS * QK_NOPE
    return pl.pallas_call(
        _cache_kv_kernel,
        grid=(n_attn, b),
        in_specs=[
            pl.BlockSpec((1, 1, past, KV_RANK), lambda a, i: (i, a, 0, 0)),
            pl.BlockSpec((1, 1, QK_ROPE, past), lambda a, i: (i, a, 0, 0)),
            pl.BlockSpec((1, KV_RANK, hk), lambda a, i: (a, 0, 0)),
            pl.BlockSpec((1, ATTN_WIDTH, KV_RANK), lambda a, i: (a, 0, 0)),
        ],
        out_specs=[
            pl.BlockSpec((1, 1, N_HEADS, past, 2 * LANES), lambda a, i: (a, i, 0, 0, 0)),
            pl.BlockSpec((1, 1, N_HEADS, V_DIM, past), lambda a, i: (a, i, 0, 0, 0)),
        ],
        out_shape=[
            jax.ShapeDtypeStruct((n_attn, b, N_HEADS, past, 2 * LANES), BF16),
            jax.ShapeDtypeStruct((n_attn, b, N_HEADS, V_DIM, past), BF16),
        ],
        compiler_params=_params(),
        name="cache_kv",
    )(cache_ckv, cache_kpe_t, w_uk, w_uvt)


def _lat_proj_kernel(x_ref, mod_ref, g_ref, cos_ref, sa_ref, sb_ref, w_in_ref, gq_ref, gkv_ref,
                     w_uq_ref, w_uk_ref, w_uvt_ref, q_ref, k_ref, vt_ref, z_ref):
    shift, scale, _ = _mod_parts(mod_ref)
    h = _norm_mod(x_ref[0], g_ref[...], shift, scale).astype(BF16)
    rope = (cos_ref[...], sa_ref[...], sb_ref[...])
    q_heads, k_nope, _, kpe_tile, v_t, _, z = _mla_project(
        h, w_in_ref, gq_ref, gkv_ref, w_uq_ref, w_uk_ref, w_uvt_ref, rope)
    _store_heads(q_ref, (0,), q_heads)
    _store_heads(k_ref, (0,), _key_tiles(k_nope, kpe_tile))
    v_t = v_t.astype(BF16)
    for hd in range(N_HEADS):
        vt_ref[0, hd] = v_t[hd * V_DIM:(hd + 1) * V_DIM, :]
    z_ref[0] = z


def _lat_proj(x, mod, mod0, layer, a, norm_g, rope, w):
    b, t, _ = x.shape
    tm = TOK_TILE
    hk = N_HEADS * QK_NOPE
    hr = N_HEADS * QK_ROPE
    row = lambda i, j: (i, j, 0)
    head_row = lambda i, j: (i, 0, j, 0)
    tab = pl.BlockSpec((tm, LANES), lambda i, j: (j, 0))
    return pl.pallas_call(
        _lat_proj_kernel,
        grid=(b, t // tm),
        in_specs=[
            pl.BlockSpec((1, tm, D), row),
            pl.BlockSpec((1, 1, 3 * D), lambda i, j: (mod0 + i, 0, 0)),
            _layer((1, D), layer),
            tab, tab, tab,
            _layer((ATTN_IN_PAD, D), a),
            _layer((1, Q_RANK), a),
            _layer((1, KV_RANK), a),
            _layer((Q_RANK, hk + hr), a),
            _layer((KV_RANK, hk), a),
            _layer((ATTN_WIDTH, KV_RANK), a),
        ],
        out_specs=[
            pl.BlockSpec((1, N_HEADS, tm, 2 * LANES), head_row),
            pl.BlockSpec((1, N_HEADS, tm, 2 * LANES), head_row),
            pl.BlockSpec((1, N_HEADS, V_DIM, tm), lambda i, j: (i, 0, 0, j)),
            pl.BlockSpec((1, tm, ATTN_WIDTH), row),
        ],
        out_shape=[
            jax.ShapeDtypeStruct((b, N_HEADS, t, 2 * LANES), BF16),
            jax.ShapeDtypeStruct((b, N_HEADS, t, 2 * LANES), BF16),
            jax.ShapeDtypeStruct((b, N_HEADS, V_DIM, t), BF16),
            jax.ShapeDtypeStruct((b, t, ATTN_WIDTH), F32),
        ],
        compiler_params=_params(),
        name="lat_proj",
    )(x, mod, norm_g, *rope, w["w_in"], w["gq"], w["gkv"], w["w_uq"], w["w_uk"], w["w_uvt"])


def _head_stage(hd_pv, hd_qk, q_ref, kc_ref, kl_ref, vtc_ref, vtl_ref, ot_ref, cur, nxt):
    past = kc_ref.shape[3]
    n_keys = past + kl_ref.shape[2]
    tq = q_ref.shape[2]
    chunks = list(range(0, n_keys, KEY_CHUNK))

    def k_of(hd, c0):
        if c0 < past:
            return kc_ref[0, 0, hd, c0:c0 + KEY_CHUNK]
        return kl_ref[0, hd, c0 - past:c0 - past + KEY_CHUNK]

    def vt_of(hd, c0):
        if c0 < past:
            return vtc_ref[0, 0, hd, :, c0:c0 + KEY_CHUNK]
        return vtl_ref[0, hd, :, c0 - past:c0 - past + KEY_CHUNK]

    col_max = []

    def scores(c0):
        s = _dot_nt(k_of(hd_qk, c0), q_h)
        nxt[0][c0:c0 + KEY_CHUNK] = s
        col_max.append(s.max(axis=0, keepdims=True))

    if hd_qk is not None:
        q_h = q_ref[0, hd_qk]
        for c0 in chunks[:QK_LEAD]:
            scores(c0)
    if hd_pv is not None:
        m = cur[1][...]
        acc = jnp.zeros((V_DIM, tq), F32)
        denom = jnp.zeros((1, tq), F32)
    for i, c0 in enumerate(chunks):
        if hd_pv is not None:
            p = jnp.exp2(cur[0][c0:c0 + KEY_CHUNK] - m)
            denom = denom + p.sum(axis=0, keepdims=True)
            acc = acc + _dot(vt_of(hd_pv, c0), p.astype(BF16))
        if hd_qk is not None and i + QK_LEAD < len(chunks):
            scores(chunks[i + QK_LEAD])
    if hd_qk is not None:
        nxt[1][...] = functools.reduce(jnp.maximum, col_max)
    if hd_pv is not None:
        ot_ref[hd_pv] = acc * (1.0 / denom)


def _lat_attn_kernel(x_ref, mod_ref, q_ref, z_ref, kc_ref, vtc_ref, kl_ref, vtl_ref, w_o_ref,
                     y_ref, s0_ref, s1_ref, m0_ref, m1_ref, ot_ref):
    _, _, gate = _mod_parts(mod_ref)
    stage = functools.partial(_head_stage, q_ref=q_ref, kc_ref=kc_ref, kl_ref=kl_ref,
                              vtc_ref=vtc_ref, vtl_ref=vtl_ref, ot_ref=ot_ref)
    bufs = ((s0_ref, m0_ref), (s1_ref, m1_ref))

    for hd in range(-1, N_HEADS):
        stage(hd if hd >= 0 else None, hd + 1 if hd + 1 < N_HEADS else None,
              cur=bufs[hd % 2], nxt=bufs[(hd + 1) % 2])

    o = jnp.concatenate([ot_ref[hd].T for hd in range(N_HEADS)], axis=1)
    y_ref[0] = _gated_residual(x_ref[0], o, z_ref[0], gate, w_o_ref)


def _lat_attn(x, mod, mod0, a, q, z, kc, vtc, kl, vtl, w_o):
    b, t, _ = x.shape
    tq = Q_TILE
    past = kc.shape[3]
    row = lambda i, j: (i, j, 0)
    per_b = lambda i, j: (i, 0, 0, 0)
    cache = lambda i, j: (a, i, 0, 0, 0)
    return pl.pallas_call(
        _lat_attn_kernel,
        grid=(b, t // tq),
        in_specs=[
            pl.BlockSpec((1, tq, D), row),
            pl.BlockSpec((1, 1, 3 * D), lambda i, j: (mod0 + i, 0, 0)),
            pl.BlockSpec((1, N_HEADS, tq, 2 * LANES), lambda i, j: (i, 0, j, 0)),
            pl.BlockSpec((1, tq, ATTN_WIDTH), row),
            pl.BlockSpec((1, 1, N_HEADS, past, 2 * LANES), cache),
            pl.BlockSpec((1, 1, N_HEADS, V_DIM, past), cache),
            pl.BlockSpec((1, N_HEADS, t, 2 * LANES), per_b),
            pl.BlockSpec((1, N_HEADS, V_DIM, t), per_b),
            _layer((ATTN_WIDTH, D), a),
        ],
        out_specs=pl.BlockSpec((1, tq, D), row),
        out_shape=jax.ShapeDtypeStruct((b, t, D), F32),
        scratch_shapes=[
            pltpu.VMEM((past + t, tq), F32), pltpu.VMEM((past + t, tq), F32),
            pltpu.VMEM((1, tq), F32), pltpu.VMEM((1, tq), F32),
            pltpu.VMEM((N_HEADS, V_DIM, tq), F32),
        ],
        compiler_params=_params(),
        name="lat_attn",
    )(x, mod, q, z, kc, vtc, kl, vtl, w_o)


def _mlp_kernel(x_ref, mod_ref, g_ref, w_in_ref, vg_ref, vb_ref, w_s_ref, b_s_ref, w_o_ref,
                fg_ref, y_ref, v_ref, uz_ref, *, final_norm):
    x = x_ref[0]
    tm = x.shape[0]
    shift, scale, gate = _mod_parts(mod_ref)
    h = _norm_mod(x, g_ref[...], shift, scale).astype(BF16)

    s1 = [jnp.zeros((tm, LANES), F32)]

    def v_group(g):
        vcols = slice(MLP_WIDTH + g * GROUP_W, MLP_WIDTH + (g + 1) * GROUP_W)
        v = jax.nn.gelu(_dot(h, w_in_ref[:, vcols]))
        v_ref[:, g * GROUP_W:(g + 1) * GROUP_W] = v
        for j in range(GROUP_W // LANES):
            s1[0] = s1[0] + v[:, j * LANES:(j + 1) * LANES]

    def uz_group(g):
        cols = slice(g * GROUP_W, (g + 1) * GROUP_W)
        zcols = slice(2 * MLP_WIDTH + g * GROUP_W, 2 * MLP_WIDTH + (g + 1) * GROUP_W)
        u = jax.nn.gelu(_dot(h, w_in_ref[:, cols]))
        z = _dot(h, w_in_ref[:, zcols])
        uz_ref[:, cols] = u * _silu(z)

    for g in range(V_LEAD):
        v_group(g)
    for g in range(MLP_GROUPS):
        if g + V_LEAD < MLP_GROUPS:
            v_group(g + V_LEAD)
        if g == MLP_GROUPS - V_LEAD:
            mu = jnp.sum(s1[0], axis=-1, keepdims=True) * (1.0 / MLP_WIDTH)
            s2 = jnp.zeros((tm, LANES), F32)
            for j in range(MLP_WIDTH // LANES):
                vc = v_ref[:, j * LANES:(j + 1) * LANES] - mu
                s2 = s2 + vc * vc
            rstd = lax.rsqrt(jnp.sum(s2, axis=-1, keepdims=True) * (1.0 / MLP_WIDTH) + EPS)
        uz_group(g)

    def spatial(g):
        cols = slice(g * GROUP_W, (g + 1) * GROUP_W)
        vn = ((v_ref[:, cols] - mu) * rstd * vg_ref[:, cols] + vb_ref[:, cols]).astype(BF16)
        bias = b_s_ref[:, g:g + 1]
        return jnp.concatenate(
            [_dot(w_s_ref[g], vn[c * CHUNK:(c + 1) * CHUNK]) + bias for c in range(tm // CHUNK)],
            axis=0)

    gates = [spatial(g) for g in range(SPATIAL_LEAD)]
    acc = None
    for g in range(MLP_GROUPS):
        cols = slice(g * GROUP_W, (g + 1) * GROUP_W)
        part = _dot((uz_ref[:, cols] * gates[g]).astype(BF16), w_o_ref[cols, :])
        acc = part if acc is None else acc + part
        if g + SPATIAL_LEAD < MLP_GROUPS:
            gates.append(spatial(g + SPATIAL_LEAD))

    y = x + gate * acc
    if final_norm:
        y = _rms(y) * fg_ref[...]
    y_ref[0] = y


def _mlp_layer(x, mod, mod0, layer, m, norm_g, w, final_g, final_norm):
    b, t, _ = x.shape
    tm = TOK_TILE
    row = lambda i, j: (i, j, 0)
    return pl.pallas_call(
        functools.partial(_mlp_kernel, final_norm=final_norm),
        grid=(b, t // tm),
        in_specs=[
            pl.BlockSpec((1, tm, D), row),
            pl.BlockSpec((1, 1, 3 * D), lambda i, j: (mod0 + i, 0, 0)),
            _layer((1, D), layer),
            _layer((D, 3 * MLP_WIDTH), m, single_buffer=True),
            _layer((1, MLP_WIDTH), m),
            _layer((1, MLP_WIDTH), m),
            _layer((MLP_GROUPS, CHUNK, CHUNK), m),
            _layer((CHUNK, MLP_GROUPS), m),
            _layer((MLP_WIDTH, D), m, single_buffer=True),
            _full((1, D)),
        ],
        out_specs=pl.BlockSpec((1, tm, D), row),
        out_shape=jax.ShapeDtypeStruct((b, t, D), F32),
        scratch_shapes=[pltpu.VMEM((tm, MLP_WIDTH), F32), pltpu.VMEM((tm, MLP_WIDTH), F32)],
        compiler_params=_params(),
        name="mlp",
    )(x, mod, norm_g, w["w_in"], w["vg"], w["vb"], w["w_s"], w["b_s_t"], w["w_o"], final_g)


def _rope_tables(n_tokens):
    pos = np.arange(n_tokens)
    inv = 1.0 / (ROPE_THETA ** (np.arange(0, AXIS_ROPE, 2, dtype=np.float64) / AXIS_ROPE))
    ang_r, ang_c = (pos // GRID_W)[:, None] * inv, (pos % GRID_W)[:, None] * inv
    cr, sr, cc, sc = np.cos(ang_r), np.sin(ang_r), np.cos(ang_c), np.sin(ang_c)
    zero = np.zeros_like(sr)
    cos = np.concatenate([cr, cr, cc, cc], axis=1)
    sin_a = np.concatenate([-sr, zero, -sc, zero], axis=1)
    sin_b = np.concatenate([zero, sr, zero, sc], axis=1)
    return tuple(jnp.asarray(np.tile(t, (1, LANES // QK_ROPE)), F32) for t in (cos, sin_a, sin_b))


def _attn_weights(w_in, gq, gkv, w_uq, w_ukv, w_o):
    n = w_in.shape[0]
    kpe0 = Q_RANK + KV_RANK
    w_in_t = jnp.swapaxes(w_in, 1, 2)
    w_in_p = jnp.concatenate(
        [w_in_t[:, :kpe0 + QK_ROPE], w_in_t[:, kpe0:kpe0 + QK_ROPE], w_in_t[:, kpe0 + QK_ROPE:]], axis=1)
    ukv = w_ukv.reshape(n, KV_RANK, N_HEADS, QK_NOPE + V_DIM)
    w_uk = ukv[..., :QK_NOPE].reshape(n, KV_RANK, -1)
    w_uvt = jnp.swapaxes(ukv[..., QK_NOPE:].reshape(n, KV_RANK, -1), 1, 2)
    return dict(w_in=w_in_p.astype(BF16), gq=gq[:, None, :], gkv=gkv[:, None, :],
                w_uq=w_uq.astype(BF16), w_uk=w_uk.astype(BF16), w_uvt=w_uvt.astype(BF16),
                w_o=w_o.astype(BF16))


def _mlp_weights(w_in_bf16, vg, vb, w_s, b_s, w_o_bf16):
    return dict(w_in=w_in_bf16, vg=vg[:, None, :], vb=vb[:, None, :],
                w_s=w_s.astype(BF16), b_s_t=jnp.swapaxes(b_s, 1, 2), w_o=w_o_bf16)


def kernel(x_prompt, x_sample, cache_ckv, cache_kpe, c, c_ctx, norm_g, w_mod, b_mod, attn_w_in, attn_q_norm_g, attn_kv_norm_g, attn_w_uq, attn_w_ukv, attn_w_o, mlp_w_in, mlp_v_norm_g, mlp_v_norm_b, mlp_w_s, mlp_b_s, mlp_w_o, final_norm_g):
    depth = norm_g.shape[0]
    bc, tc, _ = x_prompt.shape
    bl, tl, _ = x_sample.shape
    if depth % 2:
        raise NotImplementedError("final RMSNorm is fused into a trailing gMLP layer")

    cond8 = jnp.concatenate([c_ctx[None, :], c, jnp.zeros((8 - 1 - bl, D), F32)], axis=0)
    mods = _modulation(cond8, w_mod, b_mod).reshape(depth * 8, 1, 3 * D)
    rope = _rope_tables(tl)
    norm_g = norm_g[:, None, :]
    final_g = final_norm_g.reshape(1, D)
    aw = _attn_weights(attn_w_in, attn_q_norm_g, attn_kv_norm_g, attn_w_uq, attn_w_ukv, attn_w_o)
    kc, vtc = _cache_kv(cache_ckv, jnp.swapaxes(cache_kpe, 2, 3), aw["w_uk"], aw["w_uvt"])

    xc, xl = x_prompt, x_sample
    cache_out = ()
    for layer in range(depth):
        mod0 = 8 * layer
        if layer % 2 == 0:
            a = layer // 2
            if layer == 0:
                xc, *cache_out, w_in_b, w_o_b = _ctx_attn_layer(
                    xc, mods, mod0, layer, a, norm_g, aw, cache_out, riders=(mlp_w_in, mlp_w_o))
                mw = _mlp_weights(w_in_b, mlp_v_norm_g, mlp_v_norm_b, mlp_w_s, mlp_b_s, w_o_b)
            else:
                xc, *cache_out = _ctx_attn_layer(xc, mods, mod0, layer, a, norm_g, aw, cache_out)
            q, kl, vtl, z = _lat_proj(xl, mods, mod0 + 1, layer, a, norm_g, rope, aw)
            xl = _lat_attn(xl, mods, mod0 + 1, a, q, z, kc, vtc, kl, vtl, aw["w_o"])
        else:
            m = layer // 2
            last = layer == depth - 1
            xc = _mlp_layer(xc.reshape(1, bc * tc, D), mods, mod0, layer, m, norm_g, mw, final_g,
                            last).reshape(bc, tc, D)
            xl = _mlp_layer(xl, mods, mod0 + 1, layer, m, norm_g, mw, final_g, last)
    new_ckv, new_kpe_t = cache_out
    return xc, xl, new_ckv, jnp.swapaxes(new_kpe_t, 2, 3)
```

```python
import functools
import math

import jax
import jax.numpy as jnp
import numpy as np
from jax import lax
from jax.experimental import pallas as pl
from jax.experimental.pallas import tpu as pltpu

F32 = jnp.float32
BF16 = jnp.bfloat16

D = 1024
N_HEADS = 8
Q_RANK = 512
KV_RANK = 256
QK_NOPE = 128
QK_ROPE = 64
V_DIM = 128
ATTN_WIDTH = N_HEADS * V_DIM
AXIS_ROPE = QK_ROPE // 2
ROPE_THETA = 10000.0
GRID_W = 64
CHUNK = 128
MLP_GROUPS = 8
MLP_WIDTH = 2 * D
GROUP_W = MLP_WIDTH // MLP_GROUPS
EPS = 1e-6
QK_SCALE = float((QK_NOPE + QK_ROPE) ** -0.5 * math.log2(math.e))

LANES = 128
ATTN_IN_PAD = Q_RANK + KV_RANK + LANES + ATTN_WIDTH
VMEM_LIMIT = 56 * 1024 * 1024

TOK_TILE = 512
Q_TILE = 256
CTX_ROWS = 2
QK_LEAD = 3
V_LEAD = 2
SPATIAL_LEAD = 2
Z_PARTS = 4
KEY_CHUNK = 512


def _params():
    return pltpu.CompilerParams(vmem_limit_bytes=VMEM_LIMIT)


def _dot(a, b):
    return jnp.dot(a, b, preferred_element_type=F32)


def _dot_nt(a, b):
    return lax.dot_general(a, b, (((1,), (1,)), ((), ())), preferred_element_type=F32)


def _rms(x):
    return x * lax.rsqrt(jnp.mean(x * x, axis=-1, keepdims=True) + EPS)


def _silu(x):
    return x * jax.nn.sigmoid(x)


def _mod_parts(mod_ref):
    m = mod_ref[0]
    return m[:, :D], m[:, D:2 * D], m[:, 2 * D:]


def _norm_mod(x, g, shift, scale):
    return _rms(x) * (g * (1.0 + scale)) + shift


def _full(shape):
    n = len(shape)
    return pl.BlockSpec(shape, lambda *_: (0,) * n)


def _layer(shape, l, single_buffer=False):
    n = len(shape)
    mode = dict(pipeline_mode=pl.Buffered(1)) if single_buffer else {}
    return pl.BlockSpec((None,) + tuple(shape), lambda *_: (l,) + (0,) * n, **mode)


def _mod_kernel(cond_ref, w_ref, b_ref, o_ref):
    a = _silu(cond_ref[...]).astype(BF16)
    o_ref[0] = _dot(a, w_ref[0].astype(BF16)) + b_ref[0]


def _modulation(cond8, w_mod, b_mod):
    depth = w_mod.shape[0]
    tn = 1024
    return pl.pallas_call(
        _mod_kernel,
        grid=(depth, 3 * D // tn),
        in_specs=[
            pl.BlockSpec((8, D), lambda l, j: (0, 0)),
            pl.BlockSpec((1, D, tn), lambda l, j: (l, 0, j)),
            pl.BlockSpec((1, 1, tn), lambda l, j: (l, 0, j)),
        ],
        out_specs=pl.BlockSpec((1, 8, tn), lambda l, j: (l, 0, j)),
        out_shape=jax.ShapeDtypeStruct((depth, 8, 3 * D), F32),
        compiler_params=_params(),
        name="modulation",
    )(cond8, w_mod, b_mod.reshape(depth, 1, 3 * D))


def _rope(x, cos, sin_a, sin_b):
    return x * cos + pltpu.roll(x, LANES - 16, 1) * sin_a + pltpu.roll(x, 16, 1) * sin_b


def _kpe_tile(kpe2):
    lane = lax.broadcasted_iota(jnp.int32, kpe2.shape, 1)
    return jnp.where(lane < QK_ROPE, kpe2, jnp.zeros_like(kpe2))


def _q_heads(q, rope):
    lane = lax.broadcasted_iota(jnp.int32, (q.shape[0], LANES), 1)
    heads = []
    for pair in range(N_HEADS // 2):
        t0, t1, t2 = (q[:, (3 * pair + j) * LANES:(3 * pair + j + 1) * LANES] for j in range(3))
        nope_odd = jnp.where(lane < QK_ROPE, pltpu.roll(t1, QK_ROPE, 1), pltpu.roll(t2, QK_ROPE, 1))
        rot = jnp.where(lane < QK_ROPE, t1, t2)
        if rope is not None:
            rot = _rope(rot, *rope)
        heads += [(t0, rot), (nope_odd, pltpu.roll(rot, QK_ROPE, 1))]
    return heads


def _mla_project(h, w_in_ref, gq_ref, gkv_ref, w_uq_ref, w_uk_ref, w_uvt_ref, rope):
    proj = _dot_nt(h, w_in_ref[...])
    c_q = proj[:, :Q_RANK]
    c_kv = proj[:, Q_RANK:Q_RANK + KV_RANK]
    kpe2 = proj[:, Q_RANK + KV_RANK:Q_RANK + KV_RANK + LANES]
    z = proj[:, Q_RANK + KV_RANK + LANES:]
    q = _dot((_rms(c_q) * gq_ref[...]).astype(BF16), w_uq_ref[...]) * QK_SCALE
    ckv = _rms(c_kv) * gkv_ref[...]
    k_nope = _dot(ckv.astype(BF16), w_uk_ref[...])
    v_t = _dot(w_uvt_ref[...], ckv.T.astype(BF16))
    kpe_rot = kpe2 if rope is None else _rope(kpe2, *rope)
    return _q_heads(q, rope), k_nope, kpe2, _kpe_tile(kpe_rot), v_t, ckv, z


def _gated_residual(x, o, z, gate, w_o_ref):
    mix = _dot((o * _silu(z)).astype(BF16), w_o_ref[...])
    return x + gate * mix


def _ctx_attn_kernel(*refs, n_prev, n_cast):
    (x_ref, mod_ref, g_ref, w_in_ref, gq_ref, gkv_ref, w_uq_ref, w_uk_ref, w_uvt_ref,
     w_o_ref) = refs[:10]
    refs = refs[10:]
    prev, refs = refs[:2 * bool(n_prev)], refs[2 * bool(n_prev):]
    cast_in, refs = refs[:n_cast], refs[n_cast:]
    (y_ref, ckv_ref, kpet_ref), refs = refs[:3], refs[3:]
    cast_out, (s_ref, p_ref) = refs[:n_cast], refs[n_cast:]
    for src, dst in zip(cast_in, cast_out):
        dst[...] = src[...].astype(BF16)
    shift, scale, gate = _mod_parts(mod_ref)
    if n_prev:
        ckv_ref[:, :n_prev] = prev[0][...]
        kpet_ref[:, :n_prev] = prev[1][...]
    for r in range(x_ref.shape[0]):
        x = x_ref[r]
        h = _norm_mod(x, g_ref[...], shift, scale).astype(BF16)
        q_heads, k_nope, kpe2, kpe_tile, v_t, ckv, z = _mla_project(
            h, w_in_ref, gq_ref, gkv_ref, w_uq_ref, w_uk_ref, w_uvt_ref, None)
        ckv_ref[r, n_prev] = ckv
        kpet_ref[r, n_prev] = kpe2.T[:QK_ROPE]
        kpe_tile = kpe_tile.astype(BF16)
        k_nope = k_nope.astype(BF16)
        v_t = v_t.astype(BF16)
        for hd in range(N_HEADS):
            q_h = jnp.concatenate([t.astype(BF16) for t in q_heads[hd]], axis=1)
            k_h = jnp.concatenate([k_nope[:, hd * LANES:(hd + 1) * LANES], kpe_tile], axis=1)
            s_ref[r, hd] = _dot_nt(k_h, q_h)
        inv = []
        for hd in range(N_HEADS):
            s = s_ref[r, hd]
            p = jnp.exp2(s - s.max(axis=0, keepdims=True))
            inv.append(1.0 / p.sum(axis=0, keepdims=True))
            p_ref[r, hd] = p.astype(BF16)
        o_t = [_dot(v_t[hd * LANES:(hd + 1) * LANES, :], p_ref[r, hd]) * inv[hd]
               for hd in range(N_HEADS)]
        o = jnp.concatenate([t.T for t in o_t], axis=1)
        y_ref[r] = _gated_residual(x, o, z, gate, w_o_ref)


def _ctx_attn_layer(x, mod, mod0, layer, a, norm_g, w, prev, riders=()):
    b, t, _ = x.shape
    rows = CTX_ROWS
    steps = b // rows
    blk = lambda i: (i, 0, 0, 0)
    chunk = lambda p: pl.BlockSpec((p.shape[0], p.shape[1] // steps, p.shape[2]), lambda i: (0, i, 0))
    return pl.pallas_call(
        functools.partial(_ctx_attn_kernel, n_prev=a, n_cast=len(riders)),
        grid=(steps,),
        in_specs=[
            pl.BlockSpec((rows, t, D), lambda i: (i, 0, 0)),
            pl.BlockSpec((1, 1, 3 * D), lambda i: (mod0, 0, 0)),
            _layer((1, D), layer),
            _layer((ATTN_IN_PAD, D), a),
            _layer((1, Q_RANK), a),
            _layer((1, KV_RANK), a),
            _layer((Q_RANK, N_HEADS * (QK_NOPE + QK_ROPE)), a),
            _layer((KV_RANK, N_HEADS * QK_NOPE), a),
            _layer((N_HEADS * V_DIM, KV_RANK), a),
            _layer((ATTN_WIDTH, D), a),
        ] + ([pl.BlockSpec((rows, a, t, KV_RANK), blk), pl.BlockSpec((rows, a, QK_ROPE, t), blk)]
             if a else []) + [chunk(p) for p in riders],
        out_specs=[
            pl.BlockSpec((rows, t, D), lambda i: (i, 0, 0)),
            pl.BlockSpec((rows, a + 1, t, KV_RANK), blk),
            pl.BlockSpec((rows, a + 1, QK_ROPE, t), blk),
        ] + [chunk(p) for p in riders],
        out_shape=[
            jax.ShapeDtypeStruct((b, t, D), F32),
            jax.ShapeDtypeStruct((b, a + 1, t, KV_RANK), F32),
            jax.ShapeDtypeStruct((b, a + 1, QK_ROPE, t), F32),
        ] + [jax.ShapeDtypeStruct(p.shape, BF16) for p in riders],
        scratch_shapes=[pltpu.VMEM((rows, N_HEADS, t, t), F32), pltpu.VMEM((rows, N_HEADS, t, t), BF16)],
        compiler_params=_params(),
        name="ctx_attn",
    )(x, mod, norm_g, w["w_in"], w["gq"], w["gkv"], w["w_uq"], w["w_uk"], w["w_uvt"], w["w_o"], *prev,
      *riders)


def _store_heads(ref, lead, tiles):
    for hd, (nope, rot) in enumerate(tiles):
        ref[lead + (hd, slice(None), slice(0, LANES))] = nope.astype(BF16)
        ref[lead + (hd, slice(None), slice(LANES, 2 * LANES))] = rot.astype(BF16)


def _key_tiles(k_nope, kpe_tile):
    return [(k_nope[:, hd * LANES:(hd + 1) * LANES], kpe_tile) for hd in range(N_HEADS)]


def _cache_kv_kernel(ckv_ref, kpet_ref, w_uk_ref, w_uvt_ref, k_ref, vt_ref):
    ckv = ckv_ref[0, 0]
    k_nope = _dot(ckv.astype(BF16), w_uk_ref[0])
    kpe_t = kpet_ref[0, 0]
    kpe2 = jnp.concatenate([kpe_t] * (LANES // QK_ROPE), axis=0).T
    _store_heads(k_ref, (0, 0), _key_tiles(k_nope, _kpe_tile(kpe2)))
    v_t = _dot(w_uvt_ref[0], ckv.T.astype(BF16)).astype(BF16)
    for hd in range(N_HEADS):
        vt_ref[0, 0, hd] = v_t[hd * V_DIM:(hd + 1) * V_DIM, :]


def _cache_kv(cache_ckv, cache_kpe_t, w_uk, w_uvt):
    b, n_attn, past, _ = cache_ckv.shape
    hk = N_HEADS * QK_NOPE
    return pl.pallas_call(
        _cache_kv_kernel,
        grid=(n_attn, b),
        in_specs=[
            pl.BlockSpec((1, 1, past, KV_RANK), lambda a, i: (i, a, 0, 0)),
            pl.BlockSpec((1, 1, QK_ROPE, past), lambda a, i: (i, a, 0, 0)),
            pl.BlockSpec((1, KV_RANK, hk), lambda a, i: (a, 0, 0)),
            pl.BlockSpec((1, ATTN_WIDTH, KV_RANK), lambda a, i: (a, 0, 0)),
        ],
        out_specs=[
            pl.BlockSpec((1, 1, N_HEADS, past, 2 * LANES), lambda a, i: (a, i, 0, 0, 0)),
            pl.BlockSpec((1, 1, N_HEADS, V_DIM, past), lambda a, i: (a, i, 0, 0, 0)),
        ],
        out_shape=[
            jax.ShapeDtypeStruct((n_attn, b, N_HEADS, past, 2 * LANES), BF16),
            jax.ShapeDtypeStruct((n_attn, b, N_HEADS, V_DIM, past), BF16),
        ],
        compiler_params=_params(),
        name="cache_kv",
    )(cache_ckv, cache_kpe_t, w_uk, w_uvt)


def _head_stage(hd_pv, hd_qk, q_heads, k_of, vt_of, n_chunks, ot_ref, cur, nxt, after_lead=None):
    rows = lambda c: slice(c * KEY_CHUNK, (c + 1) * KEY_CHUNK)
    col_max = []

    def scores(c):
        s = _dot_nt(k_of(hd_qk, c), q_heads[hd_qk])
        nxt[0][rows(c)] = s
        col_max.append(s.max(axis=0, keepdims=True))

    if hd_qk is not None:
        for c in range(min(QK_LEAD, n_chunks)):
            scores(c)
    if after_lead is not None:
        after_lead()
    if hd_pv is not None:
        tq = cur[0].shape[1]
        m = cur[1][...]
        acc = jnp.zeros((V_DIM, tq), F32)
        denom = jnp.zeros((1, tq), F32)
    for c in range(n_chunks):
        if hd_pv is not None:
            p = jnp.exp2(cur[0][rows(c)] - m)
            denom = denom + p.sum(axis=0, keepdims=True)
            acc = acc + _dot(vt_of(hd_pv, c), p.astype(BF16))
        if hd_qk is not None and c + QK_LEAD < n_chunks:
            scores(c + QK_LEAD)
    if hd_qk is not None:
        nxt[1][...] = functools.reduce(jnp.maximum, col_max)
    if hd_pv is not None:
        ot_ref[hd_pv] = acc * (1.0 / denom)


def _lat_layer_kernel(xkv_ref, xq_ref, mod_ref, g_ref, kcos_ref, ksa_ref, ksb_ref, qcos_ref, qsa_ref, qsb_ref,
                      w_in_ref, gq_ref, gkv_ref, w_uq_ref, w_uk_ref, w_uvt_ref, w_o_ref, kc_ref, vtc_ref,
                      y_ref, k_scr, vt_scr, s0_ref, s1_ref, m0_ref, m1_ref, ot_ref, z_scr, *, n_kv_steps):
    step = pl.program_id(1)
    shift, scale, gate = _mod_parts(mod_ref)
    kv_rows = slice(Q_RANK, Q_RANK + KV_RANK + LANES)
    z_row0 = Q_RANK + KV_RANK + LANES

    @pl.when(step < n_kv_steps)
    def _keys_values():
        h = _norm_mod(xkv_ref[0], g_ref[...], shift, scale).astype(BF16)
        proj = _dot_nt(h, w_in_ref[kv_rows, :])
        ckv = _rms(proj[:, :KV_RANK]) * gkv_ref[...]
        kpe_tile = _kpe_tile(_rope(proj[:, KV_RANK:], kcos_ref[...], ksa_ref[...], ksb_ref[...]))
        k_nope = _dot(ckv.astype(BF16), w_uk_ref[...]).astype(BF16)
        v_t = _dot(w_uvt_ref[...], ckv.T.astype(BF16)).astype(BF16)
        kpe_tile = kpe_tile.astype(BF16)
        for hd in range(N_HEADS):
            k_scr[hd, step, :, :LANES] = k_nope[:, hd * LANES:(hd + 1) * LANES]
            k_scr[hd, step, :, LANES:] = kpe_tile
            vt_scr[hd, step] = v_t[hd * V_DIM:(hd + 1) * V_DIM, :]

    @pl.when(step >= n_kv_steps)
    def _queries():
        x = xq_ref[0]
        h = _norm_mod(x, g_ref[...], shift, scale).astype(BF16)
        c_q = _dot_nt(h, w_in_ref[:Q_RANK, :])
        q = _dot((_rms(c_q) * gq_ref[...]).astype(BF16), w_uq_ref[...]) * QK_SCALE
        rope = (qcos_ref[...], qsa_ref[...], qsb_ref[...])
        q_heads = [jnp.concatenate([t.astype(BF16) for t in tiles], axis=1) for tiles in _q_heads(q, rope)]

        k_of = lambda hd, c: kc_ref[0, 0, hd] if c == 0 else k_scr[hd, c - 1]
        vt_of = lambda hd, c: vtc_ref[0, 0, hd] if c == 0 else vt_scr[hd, c - 1]
        bufs = ((s0_ref, m0_ref), (s1_ref, m1_ref))
        z_cols = ATTN_WIDTH // Z_PARTS

        def z_part(i):
            z_scr[:, i * z_cols:(i + 1) * z_cols] = _dot_nt(
                h, w_in_ref[z_row0 + i * z_cols:z_row0 + (i + 1) * z_cols, :])

        for hd in range(-1, N_HEADS):
            part = hd + 1 if hd + 1 < Z_PARTS else None
            _head_stage(hd if hd >= 0 else None, hd + 1 if hd + 1 < N_HEADS else None, q_heads, k_of, vt_of,
                        n_kv_steps + 1, ot_ref, cur=bufs[hd % 2], nxt=bufs[(hd + 1) % 2],
                        after_lead=None if part is None else functools.partial(z_part, part))

        o = jnp.concatenate([ot_ref[hd].T for hd in range(N_HEADS)], axis=1)
        y_ref[0] = _gated_residual(x, o, z_scr[...], gate, w_o_ref)


def _lat_layer(x, mod, mod0, layer, a, norm_g, rope, w, kc, vtc):
    b, t, _ = x.shape
    tm, tq = KEY_CHUNK, Q_TILE
    n_kv, n_q = t // tm, t // tq
    past = kc.shape[3]
    assert past == KEY_CHUNK, "the cached context is taken as key chunk 0"
    hk = N_HEADS * QK_NOPE
    hr = N_HEADS * QK_ROPE
    kv_tile = lambda i, j: jnp.minimum(j, n_kv - 1)
    q_tile = lambda i, j: jnp.maximum(j - n_kv, 0)
    cache = lambda i, j: (a, i, 0, 0, 0)
    kv_tab = pl.BlockSpec((tm, LANES), lambda i, j: (kv_tile(i, j), 0))
    q_tab = pl.BlockSpec((tq, LANES), lambda i, j: (q_tile(i, j), 0))
    return pl.pallas_call(
        functools.partial(_lat_layer_kernel, n_kv_steps=n_kv),
        grid=(b, n_kv + n_q),
        in_specs=[
            pl.BlockSpec((1, tm, D), lambda i, j: (i, kv_tile(i, j), 0)),
            pl.BlockSpec((1, tq, D), lambda i, j: (i, q_tile(i, j), 0)),
            pl.BlockSpec((1, 1, 3 * D), lambda i, j: (mod0 + i, 0, 0)),
            _layer((1, D), layer),
            kv_tab, kv_tab, kv_tab, q_tab, q_tab, q_tab,
            _layer((ATTN_IN_PAD, D), a),
            _layer((1, Q_RANK), a),
            _layer((1, KV_RANK), a),
            _layer((Q_RANK, hk + hr), a),
            _layer((KV_RANK, hk), a),
            _layer((ATTN_WIDTH, KV_RANK), a),
            _layer((ATTN_WIDTH, D), a),
            pl.BlockSpec((1, 1, N_HEADS, past, 2 * LANES), cache),
            pl.BlockSpec((1, 1, N_HEADS, V_DIM, past), cache),
        ],
        out_specs=pl.BlockSpec((1, tq, D), lambda i, j: (i, q_tile(i, j), 0)),
        out_shape=jax.ShapeDtypeStruct((b, t, D), F32),
        scratch_shapes=[
            pltpu.VMEM((N_HEADS, n_kv, tm, 2 * LANES), BF16),
            pltpu.VMEM((N_HEADS, n_kv, V_DIM, tm), BF16),
            pltpu.VMEM((past + t, tq), F32), pltpu.VMEM((past + t, tq), F32),
            pltpu.VMEM((1, tq), F32), pltpu.VMEM((1, tq), F32),
            pltpu.VMEM((N_HEADS, V_DIM, tq), F32),
            pltpu.VMEM((tq, ATTN_WIDTH), F32),
        ],
        compiler_params=_params(),
        name="lat_layer",
    )(x, x, mod, norm_g, *rope, *rope, w["w_in"], w["gq"], w["gkv"], w["w_uq"], w["w_uk"], w["w_uvt"], w["w_o"],
      kc, vtc)


def _mlp_kernel(x_ref, mod_ref, g_ref, w_in_ref, vg_ref, vb_ref, w_s_ref, b_s_ref, w_o_ref,
                fg_ref, y_ref, v_ref, uz_ref, *, final_norm):
    x = x_ref[0]
    tm = x.shape[0]
    shift, scale, gate = _mod_parts(mod_ref)
    h = _norm_mod(x, g_ref[...], shift, scale).astype(BF16)

    s1 = [jnp.zeros((tm, LANES), F32)]

    def v_group(g):
        vcols = slice(MLP_WIDTH + g * GROUP_W, MLP_WIDTH + (g + 1) * GROUP_W)
        v = jax.nn.gelu(_dot(h, w_in_ref[:, vcols]))
        v_ref[:, g * GROUP_W:(g + 1) * GROUP_W] = v
        for j in range(GROUP_W // LANES):
            s1[0] = s1[0] + v[:, j * LANES:(j + 1) * LANES]

    def uz_group(g):
        cols = slice(g * GROUP_W, (g + 1) * GROUP_W)
        zcols = slice(2 * MLP_WIDTH + g * GROUP_W, 2 * MLP_WIDTH + (g + 1) * GROUP_W)
        u = jax.nn.gelu(_dot(h, w_in_ref[:, cols]))
        z = _dot(h, w_in_ref[:, zcols])
        uz_ref[:, cols] = u * _silu(z)

    for g in range(V_LEAD):
        v_group(g)
    for g in range(MLP_GROUPS):
        if g + V_LEAD < MLP_GROUPS:
            v_group(g + V_LEAD)
        if g == MLP_GROUPS - V_LEAD:
            mu = jnp.sum(s1[0], axis=-1, keepdims=True) * (1.0 / MLP_WIDTH)
            s2 = jnp.zeros((tm, LANES), F32)
            for j in range(MLP_WIDTH // LANES):
                vc = v_ref[:, j * LANES:(j + 1) * LANES] - mu
                s2 = s2 + vc * vc
            rstd = lax.rsqrt(jnp.sum(s2, axis=-1, keepdims=True) * (1.0 / MLP_WIDTH) + EPS)
        uz_group(g)

    def spatial(g):
        cols = slice(g * GROUP_W, (g + 1) * GROUP_W)
        vn = ((v_ref[:, cols] - mu) * rstd * vg_ref[:, cols] + vb_ref[:, cols]).astype(BF16)
        bias = b_s_ref[:, g:g + 1]
        return jnp.concatenate(
            [_dot(w_s_ref[g], vn[c * CHUNK:(c + 1) * CHUNK]) + bias for c in range(tm // CHUNK)],
            axis=0)

    gates = [spatial(g) for g in range(SPATIAL_LEAD)]
    acc = None
    for g in range(MLP_GROUPS):
        cols = slice(g * GROUP_W, (g + 1) * GROUP_W)
        part = _dot((uz_ref[:, cols] * gates[g]).astype(BF16), w_o_ref[cols, :])
        acc = part if acc is None else acc + part
        if g + SPATIAL_LEAD < MLP_GROUPS:
            gates.append(spatial(g + SPATIAL_LEAD))

    y = x + gate * acc
    if final_norm:
        y = _rms(y) * fg_ref[...]
    y_ref[0] = y


def _mlp_layer(x, mod, mod0, layer, m, norm_g, w, final_g, final_norm):
    b, t, _ = x.shape
    tm = TOK_TILE
    row = lambda i, j: (i, j, 0)
    return pl.pallas_call(
        functools.partial(_mlp_kernel, final_norm=final_norm),
        grid=(b, t // tm),
        in_specs=[
            pl.BlockSpec((1, tm, D), row),
            pl.BlockSpec((1, 1, 3 * D), lambda i, j: (mod0 + i, 0, 0)),
            _layer((1, D), layer),
            _layer((D, 3 * MLP_WIDTH), m, single_buffer=True),
            _layer((1, MLP_WIDTH), m),
            _layer((1, MLP_WIDTH), m),
            _layer((MLP_GROUPS, CHUNK, CHUNK), m),
            _layer((CHUNK, MLP_GROUPS), m),
            _layer((MLP_WIDTH, D), m, single_buffer=True),
            _full((1, D)),
        ],
        out_specs=pl.BlockSpec((1, tm, D), row),
        out_shape=jax.ShapeDtypeStruct((b, t, D), F32),
        scratch_shapes=[pltpu.VMEM((tm, MLP_WIDTH), F32), pltpu.VMEM((tm, MLP_WIDTH), F32)],
        compiler_params=_params(),
        name="mlp",
    )(x, mod, norm_g, w["w_in"], w["vg"], w["vb"], w["w_s"], w["b_s_t"], w["w_o"], final_g)


def _rope_tables(n_tokens):
    pos = np.arange(n_tokens)
    inv = 1.0 / (ROPE_THETA ** (np.arange(0, AXIS_ROPE, 2, dtype=np.float64) / AXIS_ROPE))
    ang_r, ang_c = (pos // GRID_W)[:, None] * inv, (pos % GRID_W)[:, None] * inv
    cr, sr, cc, sc = np.cos(ang_r), np.sin(ang_r), np.cos(ang_c), np.sin(ang_c)
    zero = np.zeros_like(sr)
    cos = np.concatenate([cr, cr, cc, cc], axis=1)
    sin_a = np.concatenate([-sr, zero, -sc, zero], axis=1)
    sin_b = np.concatenate([zero, sr, zero, sc], axis=1)
    return tuple(jnp.asarray(np.tile(t, (1, LANES // QK_ROPE)), F32) for t in (cos, sin_a, sin_b))


def _attn_weights(w_in, gq, gkv, w_uq, w_ukv, w_o):
    n = w_in.shape[0]
    kpe0 = Q_RANK + KV_RANK
    w_in_t = jnp.swapaxes(w_in, 1, 2)
    w_in_p = jnp.concatenate(
        [w_in_t[:, :kpe0 + QK_ROPE], w_in_t[:, kpe0:kpe0 + QK_ROPE], w_in_t[:, kpe0 + QK_ROPE:]], axis=1)
    ukv = w_ukv.reshape(n, KV_RANK, N_HEADS, QK_NOPE + V_DIM)
    w_uk = ukv[..., :QK_NOPE].reshape(n, KV_RANK, -1)
    w_uvt = jnp.swapaxes(ukv[..., QK_NOPE:].reshape(n, KV_RANK, -1), 1, 2)
    return dict(w_in=w_in_p.astype(BF16), gq=gq[:, None, :], gkv=gkv[:, None, :],
                w_uq=w_uq.astype(BF16), w_uk=w_uk.astype(BF16), w_uvt=w_uvt.astype(BF16),
                w_o=w_o.astype(BF16))


def _mlp_weights(w_in_bf16, vg, vb, w_s, b_s, w_o_bf16):
    return dict(w_in=w_in_bf16, vg=vg[:, None, :], vb=vb[:, None, :],
                w_s=w_s.astype(BF16), b_s_t=jnp.swapaxes(b_s, 1, 2), w_o=w_o_bf16)


def kernel(x_prompt, x_sample, cache_ckv, cache_kpe, c, c_ctx, norm_g, w_mod, b_mod, attn_w_in, attn_q_norm_g, attn_kv_norm_g, attn_w_uq, attn_w_ukv, attn_w_o, mlp_w_in, mlp_v_norm_g, mlp_v_norm_b, mlp_w_s, mlp_b_s, mlp_w_o, final_norm_g):
    depth = norm_g.shape[0]
    bc, tc, _ = x_prompt.shape
    bl, tl, _ = x_sample.shape
    if depth % 2:
        raise NotImplementedError("final RMSNorm is fused into a trailing gMLP layer")

    cond8 = jnp.concatenate([c_ctx[None, :], c, jnp.zeros((8 - 1 - bl, D), F32)], axis=0)
    mods = _modulation(cond8, w_mod, b_mod).reshape(depth * 8, 1, 3 * D)
    rope = _rope_tables(tl)
    norm_g = norm_g[:, None, :]
    final_g = final_norm_g.reshape(1, D)
    aw = _attn_weights(attn_w_in, attn_q_norm_g, attn_kv_norm_g, attn_w_uq, attn_w_ukv, attn_w_o)
    kc, vtc = _cache_kv(cache_ckv, jnp.swapaxes(cache_kpe, 2, 3), aw["w_uk"], aw["w_uvt"])

    xc, xl = x_prompt, x_sample
    cache_out = ()
    for layer in range(depth):
        mod0 = 8 * layer
        if layer % 2 == 0:
            a = layer // 2
            if layer == 0:
                xc, *cache_out, w_in_b, w_o_b = _ctx_attn_layer(
                    xc, mods, mod0, layer, a, norm_g, aw, cache_out, riders=(mlp_w_in, mlp_w_o))
                mw = _mlp_weights(w_in_b, mlp_v_norm_g, mlp_v_norm_b, mlp_w_s, mlp_b_s, w_o_b)
            else:
                xc, *cache_out = _ctx_attn_layer(xc, mods, mod0, layer, a, norm_g, aw, cache_out)
            xl = _lat_layer(xl, mods, mod0 + 1, layer, a, norm_g, rope, aw, kc, vtc)
        else:
            m = layer // 2
            last = layer == depth - 1
            xc = _mlp_layer(xc.reshape(1, bc * tc, D), mods, mod0, layer, m, norm_g, mw, final_g,
                            last).reshape(bc, tc, D)
            xl = _mlp_layer(xl, mods, mod0 + 1, layer, m, norm_g, mw, final_g, last)
    new_ckv, new_kpe_t = cache_out
    return xc, xl, new_ckv, jnp.swapaxes(new_kpe_t, 2, 3)
```

```python
import functools
import math

import jax
import jax.numpy as jnp
import numpy as np
from jax import lax
from jax.experimental import pallas as pl
from jax.experimental.pallas import tpu as pltpu

F32 = jnp.float32
BF16 = jnp.bfloat16

D = 1024
N_HEADS = 8
Q_RANK = 512
KV_RANK = 256
QK_NOPE = 128
QK_ROPE = 64
V_DIM = 128
ATTN_WIDTH = N_HEADS * V_DIM
AXIS_ROPE = QK_ROPE // 2
ROPE_THETA = 10000.0
GRID_W = 64
CHUNK = 128
MLP_GROUPS = 8
MLP_WIDTH = 2 * D
GROUP_W = MLP_WIDTH // MLP_GROUPS
EPS = 1e-6
QK_SCALE = float((QK_NOPE + QK_ROPE) ** -0.5 * math.log2(math.e))

LANES = 128
ATTN_IN_PAD = Q_RANK + KV_RANK + LANES + ATTN_WIDTH
VMEM_LIMIT = 56 * 1024 * 1024

TOK_TILE = 512
Q_TILE = 256
CTX_ROWS = 2
QK_LEAD = 3
V_LEAD = 2
SPATIAL_LEAD = 2
KEY_CHUNK = 512


def _params():
    return pltpu.CompilerParams(vmem_limit_bytes=VMEM_LIMIT)


def _dot(a, b):
    return jnp.dot(a, b, preferred_element_type=F32)


def _dot_nt(a, b):
    return lax.dot_general(a, b, (((1,), (1,)), ((), ())), preferred_element_type=F32)


def _rms(x):
    return x * lax.rsqrt(jnp.mean(x * x, axis=-1, keepdims=True) + EPS)


def _silu(x):
    return x * jax.nn.sigmoid(x)


def _mod_parts(mod_ref):
    m = mod_ref[0]
    return m[:, :D], m[:, D:2 * D], m[:, 2 * D:]


def _norm_mod(x, g, shift, scale):
    return _rms(x) * (g * (1.0 + scale)) + shift


def _full(shape):
    n = len(shape)
    return pl.BlockSpec(shape, lambda *_: (0,) * n)


def _layer(shape, l, single_buffer=False):
    n = len(shape)
    mode = dict(pipeline_mode=pl.Buffered(1)) if single_buffer else {}
    return pl.BlockSpec((None,) + tuple(shape), lambda *_: (l,) + (0,) * n, **mode)


def _mod_kernel(cond_ref, w_ref, b_ref, o_ref):
    a = _silu(cond_ref[...]).astype(BF16)
    o_ref[0] = _dot(a, w_ref[0].astype(BF16)) + b_ref[0]


def _modulation(cond8, w_mod, b_mod):
    depth = w_mod.shape[0]
    tn = 1024
    return pl.pallas_call(
        _mod_kernel,
        grid=(depth, 3 * D // tn),
        in_specs=[
            pl.BlockSpec((8, D), lambda l, j: (0, 0)),
            pl.BlockSpec((1, D, tn), lambda l, j: (l, 0, j)),
            pl.BlockSpec((1, 1, tn), lambda l, j: (l, 0, j)),
        ],
        out_specs=pl.BlockSpec((1, 8, tn), lambda l, j: (l, 0, j)),
        out_shape=jax.ShapeDtypeStruct((depth, 8, 3 * D), F32),
        compiler_params=_params(),
        name="modulation",
    )(cond8, w_mod, b_mod.reshape(depth, 1, 3 * D))


def _rope(x, cos, sin_a, sin_b):
    return x * cos + pltpu.roll(x, LANES - 16, 1) * sin_a + pltpu.roll(x, 16, 1) * sin_b


def _kpe_tile(kpe2):
    lane = lax.broadcasted_iota(jnp.int32, kpe2.shape, 1)
    return jnp.where(lane < QK_ROPE, kpe2, jnp.zeros_like(kpe2))


def _q_heads(q, rope):
    lane = lax.broadcasted_iota(jnp.int32, (q.shape[0], LANES), 1)
    heads = []
    for pair in range(N_HEADS // 2):
        t0, t1, t2 = (q[:, (3 * pair + j) * LANES:(3 * pair + j + 1) * LANES] for j in range(3))
        nope_odd = jnp.where(lane < QK_ROPE, pltpu.roll(t1, QK_ROPE, 1), pltpu.roll(t2, QK_ROPE, 1))
        rot = jnp.where(lane < QK_ROPE, t1, t2)
        if rope is not None:
            rot = _rope(rot, *rope)
        heads += [(t0, rot), (nope_odd, pltpu.roll(rot, QK_ROPE, 1))]
    return heads


def _mla_project(h, w_in_ref, gq_ref, gkv_ref, w_uq_ref, w_uk_ref, w_uvt_ref, rope, defer_z=False):
    z_row0 = Q_RANK + KV_RANK + LANES
    proj = _dot_nt(h, w_in_ref[:z_row0 if defer_z else ATTN_IN_PAD, :])
    c_q = proj[:, :Q_RANK]
    c_kv = proj[:, Q_RANK:Q_RANK + KV_RANK]
    kpe2 = proj[:, Q_RANK + KV_RANK:z_row0]
    z = (lambda: _dot_nt(h, w_in_ref[z_row0:, :])) if defer_z else (lambda: proj[:, z_row0:])
    q = _dot((_rms(c_q) * gq_ref[...]).astype(BF16), w_uq_ref[...]) * QK_SCALE
    ckv = _rms(c_kv) * gkv_ref[...]
    k_nope = _dot(ckv.astype(BF16), w_uk_ref[...])
    v_t = _dot(w_uvt_ref[...], ckv.T.astype(BF16))
    kpe_rot = kpe2 if rope is None else _rope(kpe2, *rope)
    return _q_heads(q, rope), k_nope, kpe2, _kpe_tile(kpe_rot), v_t, ckv, z


def _gated_residual(x, o, z, gate, w_o_ref):
    mix = _dot((o * _silu(z)).astype(BF16), w_o_ref[...])
    return x + gate * mix


def _ctx_attn_kernel(*refs, n_prev, n_cast):
    (x_ref, mod_ref, g_ref, w_in_ref, gq_ref, gkv_ref, w_uq_ref, w_uk_ref, w_uvt_ref,
     w_o_ref) = refs[:10]
    refs = refs[10:]
    prev, refs = refs[:2 * bool(n_prev)], refs[2 * bool(n_prev):]
    cast_in, refs = refs[:n_cast], refs[n_cast:]
    (y_ref, ckv_ref, kpet_ref), refs = refs[:3], refs[3:]
    cast_out, (s_ref, p_ref) = refs[:n_cast], refs[n_cast:]
    for src, dst in zip(cast_in, cast_out):
        dst[...] = src[...].astype(BF16)
    shift, scale, gate = _mod_parts(mod_ref)
    if n_prev:
        ckv_ref[:, :n_prev] = prev[0][...]
        kpet_ref[:, :n_prev] = prev[1][...]
    for r in range(x_ref.shape[0]):
        x = x_ref[r]
        h = _norm_mod(x, g_ref[...], shift, scale).astype(BF16)
        q_heads, k_nope, kpe2, kpe_tile, v_t, ckv, z = _mla_project(
            h, w_in_ref, gq_ref, gkv_ref, w_uq_ref, w_uk_ref, w_uvt_ref, None, defer_z=True)
        ckv_ref[r, n_prev] = ckv
        kpet_ref[r, n_prev] = kpe2.T[:QK_ROPE]
        kpe_tile = kpe_tile.astype(BF16)
        k_nope = k_nope.astype(BF16)
        v_t = v_t.astype(BF16)
        for hd in range(N_HEADS):
            q_h = jnp.concatenate([t.astype(BF16) for t in q_heads[hd]], axis=1)
            k_h = jnp.concatenate([k_nope[:, hd * LANES:(hd + 1) * LANES], kpe_tile], axis=1)
            s_ref[r, hd] = _dot_nt(k_h, q_h)
        z = z()
        inv = []
        for hd in range(N_HEADS):
            s = s_ref[r, hd]
            p = jnp.exp2(s - s.max(axis=0, keepdims=True))
            inv.append(1.0 / p.sum(axis=0, keepdims=True))
            p_ref[r, hd] = p.astype(BF16)
        o_t = [_dot(v_t[hd * LANES:(hd + 1) * LANES, :], p_ref[r, hd]) * inv[hd]
               for hd in range(N_HEADS)]
        o = jnp.concatenate([t.T for t in o_t], axis=1)
        y_ref[r] = _gated_residual(x, o, z, gate, w_o_ref)


def _ctx_attn_layer(x, mod, mod0, layer, a, norm_g, w, prev, riders=()):
    b, t, _ = x.shape
    rows = CTX_ROWS
    steps = b // rows
    blk = lambda i: (i, 0, 0, 0)
    chunk = lambda p: pl.BlockSpec((p.shape[0], p.shape[1] // steps, p.shape[2]), lambda i: (0, i, 0))
    return pl.pallas_call(
        functools.partial(_ctx_attn_kernel, n_prev=a, n_cast=len(riders)),
        grid=(steps,),
        in_specs=[
            pl.BlockSpec((rows, t, D), lambda i: (i, 0, 0)),
            pl.BlockSpec((1, 1, 3 * D), lambda i: (mod0, 0, 0)),
            _layer((1, D), layer),
            _layer((ATTN_IN_PAD, D), a),
            _layer((1, Q_RANK), a),
            _layer((1, KV_RANK), a),
            _layer((Q_RANK, N_HEADS * (QK_NOPE + QK_ROPE)), a),
            _layer((KV_RANK, N_HEADS * QK_NOPE), a),
            _layer((N_HEADS * V_DIM, KV_RANK), a),
            _layer((ATTN_WIDTH, D), a),
        ] + ([pl.BlockSpec((rows, a, t, KV_RANK), blk), pl.BlockSpec((rows, a, QK_ROPE, t), blk)]
             if a else []) + [chunk(p) for p in riders],
        out_specs=[
            pl.BlockSpec((rows, t, D), lambda i: (i, 0, 0)),
            pl.BlockSpec((rows, a + 1, t, KV_RANK), blk),
            pl.BlockSpec((rows, a + 1, QK_ROPE, t), blk),
        ] + [chunk(p) for p in riders],
        out_shape=[
            jax.ShapeDtypeStruct((b, t, D), F32),
            jax.ShapeDtypeStruct((b, a + 1, t, KV_RANK), F32),
            jax.ShapeDtypeStruct((b, a + 1, QK_ROPE, t), F32),
        ] + [jax.ShapeDtypeStruct(p.shape, BF16) for p in riders],
        scratch_shapes=[pltpu.VMEM((rows, N_HEADS, t, t), F32), pltpu.VMEM((rows, N_HEADS, t, t), BF16)],
        compiler_params=_params(),
        name="ctx_attn",
    )(x, mod, norm_g, w["w_in"], w["gq"], w["gkv"], w["w_uq"], w["w_uk"], w["w_uvt"], w["w_o"], *prev,
      *riders)


def _store_heads(ref, lead, tiles):
    for hd, (nope, rot) in enumerate(tiles):
        ref[lead + (hd, slice(None), slice(0, LANES))] = nope.astype(BF16)
        ref[lead + (hd, slice(None), slice(LANES, 2 * LANES))] = rot.astype(BF16)


def _key_tiles(k_nope, kpe_tile):
    return [(k_nope[:, hd * LANES:(hd + 1) * LANES], kpe_tile) for hd in range(N_HEADS)]


def _cache_kv_kernel(ckv_ref, kpet_ref, w_uk_ref, w_uvt_ref, k_ref, vt_ref):
    ckv = ckv_ref[0, 0]
    k_nope = _dot(ckv.astype(BF16), w_uk_ref[0])
    kpe_t = kpet_ref[0, 0]
    kpe2 = jnp.concatenate([kpe_t] * (LANES // QK_ROPE), axis=0).T
    _store_heads(k_ref, (0, 0), _key_tiles(k_nope, _kpe_tile(kpe2)))
    v_t = _dot(w_uvt_ref[0], ckv.T.astype(BF16)).astype(BF16)
    for hd in range(N_HEADS):
        vt_ref[0, 0, hd] = v_t[hd * V_DIM:(hd + 1) * V_DIM, :]


def _cache_kv(cache_ckv, cache_kpe_t, w_uk, w_uvt):
    b, n_attn, past, _ = cache_ckv.shape
    hk = N_HEADS * QK_NOPE
    return pl.pallas_call(
        _cache_kv_kernel,
        grid=(n_attn, b),
        in_specs=[
            pl.BlockSpec((1, 1, past, KV_RANK), lambda a, i: (i, a, 0, 0)),
            pl.BlockSpec((1, 1, QK_ROPE, past), lambda a, i: (i, a, 0, 0)),
            pl.BlockSpec((1, KV_RANK, hk), lambda a, i: (a, 0, 0)),
            pl.BlockSpec((1, ATTN_WIDTH, KV_RANK), lambda a, i: (a, 0, 0)),
        ],
        out_specs=[
            pl.BlockSpec((1, 1, N_HEADS, past, 2 * LANES), lambda a, i: (a, i, 0, 0, 0)),
            pl.BlockSpec((1, 1, N_HEADS, V_DIM, past), lambda a, i: (a, i, 0, 0, 0)),
        ],
        out_shape=[
            jax.ShapeDtypeStruct((n_attn, b, N_HEADS, past, 2 * LANES), BF16),
            jax.ShapeDtypeStruct((n_attn, b, N_HEADS, V_DIM, past), BF16),
        ],
        compiler_params=_params(),
        name="cache_kv",
    )(cache_ckv, cache_kpe_t, w_uk, w_uvt)


def _lat_proj_kernel(x_ref, mod_ref, g_ref, cos_ref, sa_ref, sb_ref, w_in_ref, gq_ref, gkv_ref,
                     w_uq_ref, w_uk_ref, w_uvt_ref, q_ref, k_ref, vt_ref, z_ref):
    shift, scale, _ = _mod_parts(mod_ref)
    h = _norm_mod(x_ref[0], g_ref[...], shift, scale).astype(BF16)
    rope = (cos_ref[...], sa_ref[...], sb_ref[...])
    q_heads, k_nope, _, kpe_tile, v_t, _, z = _mla_project(
        h, w_in_ref, gq_ref, gkv_ref, w_uq_ref, w_uk_ref, w_uvt_ref, rope)
    _store_heads(q_ref, (0,), q_heads)
    _store_heads(k_ref, (0,), _key_tiles(k_nope, kpe_tile))
    v_t = v_t.astype(BF16)
    for hd in range(N_HEADS):
        vt_ref[0, hd] = v_t[hd * V_DIM:(hd + 1) * V_DIM, :]
    z_ref[0] = z()


def _lat_proj(x, mod, mod0, layer, a, norm_g, rope, w):
    b, t, _ = x.shape
    tm = TOK_TILE
    hk = N_HEADS * QK_NOPE
    hr = N_HEADS * QK_ROPE
    row = lambda i, j: (i, j, 0)
    head_row = lambda i, j: (i, 0, j, 0)
    tab = pl.BlockSpec((tm, LANES), lambda i, j: (j, 0))
    return pl.pallas_call(
        _lat_proj_kernel,
        grid=(b, t // tm),
        in_specs=[
            pl.BlockSpec((1, tm, D), row),
            pl.BlockSpec((1, 1, 3 * D), lambda i, j: (mod0 + i, 0, 0)),
            _layer((1, D), layer),
            tab, tab, tab,
            _layer((ATTN_IN_PAD, D), a),
            _layer((1, Q_RANK), a),
            _layer((1, KV_RANK), a),
            _layer((Q_RANK, hk + hr), a),
            _layer((KV_RANK, hk), a),
            _layer((ATTN_WIDTH, KV_RANK), a),
        ],
        out_specs=[
            pl.BlockSpec((1, N_HEADS, tm, 2 * LANES), head_row),
            pl.BlockSpec((1, N_HEADS, tm, 2 * LANES), head_row),
            pl.BlockSpec((1, N_HEADS, V_DIM, tm), lambda i, j: (i, 0, 0, j)),
            pl.BlockSpec((1, tm, ATTN_WIDTH), row),
        ],
        out_shape=[
            jax.ShapeDtypeStruct((b, N_HEADS, t, 2 * LANES), BF16),
            jax.ShapeDtypeStruct((b, N_HEADS, t, 2 * LANES), BF16),
            jax.ShapeDtypeStruct((b, N_HEADS, V_DIM, t), BF16),
            jax.ShapeDtypeStruct((b, t, ATTN_WIDTH), F32),
        ],
        compiler_params=_params(),
        name="lat_proj",
    )(x, mod, norm_g, *rope, w["w_in"], w["gq"], w["gkv"], w["w_uq"], w["w_uk"], w["w_uvt"])


def _head_stage(hd_pv, hd_qk, q_ref, kc_ref, kl_ref, vtc_ref, vtl_ref, ot_ref, cur, nxt):
    past = kc_ref.shape[3]
    n_keys = past + kl_ref.shape[2]
    tq = q_ref.shape[2]
    chunks = list(range(0, n_keys, KEY_CHUNK))

    def k_of(hd, c0):
        if c0 < past:
            return kc_ref[0, 0, hd, c0:c0 + KEY_CHUNK]
        return kl_ref[0, hd, c0 - past:c0 - past + KEY_CHUNK]

    def vt_of(hd, c0):
        if c0 < past:
            return vtc_ref[0, 0, hd, :, c0:c0 + KEY_CHUNK]
        return vtl_ref[0, hd, :, c0 - past:c0 - past + KEY_CHUNK]

    col_max = []

    def scores(c0):
        s = _dot_nt(k_of(hd_qk, c0), q_h)
        nxt[0][c0:c0 + KEY_CHUNK] = s
        col_max.append(s.max(axis=0, keepdims=True))

    if hd_qk is not None:
        q_h = q_ref[0, hd_qk]
        for c0 in chunks[:QK_LEAD]:
            scores(c0)
    if hd_pv is not None:
        m = cur[1][...]
        acc = jnp.zeros((V_DIM, tq), F32)
        denom = jnp.zeros((1, tq), F32)
    for i, c0 in enumerate(chunks):
        if hd_pv is not None:
            p = jnp.exp2(cur[0][c0:c0 + KEY_CHUNK] - m)
            denom = denom + p.sum(axis=0, keepdims=True)
            acc = acc + _dot(vt_of(hd_pv, c0), p.astype(BF16))
        if hd_qk is not None and i + QK_LEAD < len(chunks):
            scores(chunks[i + QK_LEAD])
    if hd_qk is not None:
        nxt[1][...] = functools.reduce(jnp.maximum, col_max)
    if hd_pv is not None:
        ot_ref[hd_pv] = acc * (1.0 / denom)


def _lat_attn_kernel(x_ref, mod_ref, q_ref, z_ref, kc_ref, vtc_ref, kl_ref, vtl_ref, w_o_ref,
                     y_ref, s0_ref, s1_ref, m0_ref, m1_ref, ot_ref):
    _, _, gate = _mod_parts(mod_ref)
    stage = functools.partial(_head_stage, q_ref=q_ref, kc_ref=kc_ref, kl_ref=kl_ref,
                              vtc_ref=vtc_ref, vtl_ref=vtl_ref, ot_ref=ot_ref)
    bufs = ((s0_ref, m0_ref), (s1_ref, m1_ref))

    for hd in range(-1, N_HEADS):
        stage(hd if hd >= 0 else None, hd + 1 if hd + 1 < N_HEADS else None,
              cur=bufs[hd % 2], nxt=bufs[(hd + 1) % 2])

    o = jnp.concatenate([ot_ref[hd].T for hd in range(N_HEADS)], axis=1)
    y_ref[0] = _gated_residual(x_ref[0], o, z_ref[0], gate, w_o_ref)


def _lat_attn(x, mod, mod0, a, q, z, kc, vtc, kl, vtl, w_o):
    b, t, _ = x.shape
    tq = Q_TILE
    past = kc.shape[3]
    row = lambda i, j: (i, j, 0)
    per_b = lambda i, j: (i, 0, 0, 0)
    cache = lambda i, j: (a, i, 0, 0, 0)
    return pl.pallas_call(
        _lat_attn_kernel,
        grid=(b, t // tq),
        in_specs=[
            pl.BlockSpec((1, tq, D), row),
            pl.BlockSpec((1, 1, 3 * D), lambda i, j: (mod0 + i, 0, 0)),
            pl.BlockSpec((1, N_HEADS, tq, 2 * LANES), lambda i, j: (i, 0, j, 0)),
            pl.BlockSpec((1, tq, ATTN_WIDTH), row),
            pl.BlockSpec((1, 1, N_HEADS, past, 2 * LANES), cache),
            pl.BlockSpec((1, 1, N_HEADS, V_DIM, past), cache),
            pl.BlockSpec((1, N_HEADS, t, 2 * LANES), per_b),
            pl.BlockSpec((1, N_HEADS, V_DIM, t), per_b),
            _layer((ATTN_WIDTH, D), a),
        ],
        out_specs=pl.BlockSpec((1, tq, D), row),
        out_shape=jax.ShapeDtypeStruct((b, t, D), F32),
        scratch_shapes=[
            pltpu.VMEM((past + t, tq), F32), pltpu.VMEM((past + t, tq), F32),
            pltpu.VMEM((1, tq), F32), pltpu.VMEM((1, tq), F32),
            pltpu.VMEM((N_HEADS, V_DIM, tq), F32),
        ],
        compiler_params=_params(),
        name="lat_attn",
    )(x, mod, q, z, kc, vtc, kl, vtl, w_o)


def _mlp_kernel(x_ref, mod_ref, g_ref, w_in_ref, vg_ref, vb_ref, w_s_ref, b_s_ref, w_o_ref,
                fg_ref, y_ref, v_ref, uz_ref, *, final_norm):
    x = x_ref[0]
    tm = x.shape[0]
    shift, scale, gate = _mod_parts(mod_ref)
    h = _norm_mod(x, g_ref[...], shift, scale).astype(BF16)

    s1 = [jnp.zeros((tm, LANES), F32)]

    def v_group(g):
        vcols = slice(MLP_WIDTH + g * GROUP_W, MLP_WIDTH + (g + 1) * GROUP_W)
        v = jax.nn.gelu(_dot(h, w_in_ref[:, vcols]))
        v_ref[:, g * GROUP_W:(g + 1) * GROUP_W] = v
        for j in range(GROUP_W // LANES):
            s1[0] = s1[0] + v[:, j * LANES:(j + 1) * LANES]

    def uz_group(g):
        cols = slice(g * GROUP_W, (g + 1) * GROUP_W)
        zcols = slice(2 * MLP_WIDTH + g * GROUP_W, 2 * MLP_WIDTH + (g + 1) * GROUP_W)
        u = jax.nn.gelu(_dot(h, w_in_ref[:, cols]))
        z = _dot(h, w_in_ref[:, zcols])
        uz_ref[:, cols] = u * _silu(z)

    for g in range(V_LEAD):
        v_group(g)
    for g in range(MLP_GROUPS):
        if g + V_LEAD < MLP_GROUPS:
            v_group(g + V_LEAD)
        if g == MLP_GROUPS - V_LEAD:
            mu = jnp.sum(s1[0], axis=-1, keepdims=True) * (1.0 / MLP_WIDTH)
            s2 = jnp.zeros((tm, LANES), F32)
            for j in range(MLP_WIDTH // LANES):
                vc = v_ref[:, j * LANES:(j + 1) * LANES] - mu
                s2 = s2 + vc * vc
            rstd = lax.rsqrt(jnp.sum(s2, axis=-1, keepdims=True) * (1.0 / MLP_WIDTH) + EPS)
        uz_group(g)

    def spatial(g):
        cols = slice(g * GROUP_W, (g + 1) * GROUP_W)
        vn = ((v_ref[:, cols] - mu) * rstd * vg_ref[:, cols] + vb_ref[:, cols]).astype(BF16)
        bias = b_s_ref[:, g:g + 1]
        return jnp.concatenate(
            [_dot(w_s_ref[g], vn[c * CHUNK:(c + 1) * CHUNK]) + bias for c in range(tm // CHUNK)],
            axis=0)

    gates = [spatial(g) for g in range(SPATIAL_LEAD)]
    acc = None
    for g in range(MLP_GROUPS):
        cols = slice(g * GROUP_W, (g + 1) * GROUP_W)
        part = _dot((uz_ref[:, cols] * gates[g]).astype(BF16), w_o_ref[cols, :])
        acc = part if acc is None else acc + part
        if g + SPATIAL_LEAD < MLP_GROUPS:
            gates.append(spatial(g + SPATIAL_LEAD))

    y = x + gate * acc
    if final_norm:
        y = _rms(y) * fg_ref[...]
    y_ref[0] = y


def _mlp_layer(x, mod, mod0, layer, m, norm_g, w, final_g, final_norm):
    b, t, _ = x.shape
    tm = TOK_TILE
    row = lambda i, j: (i, j, 0)
    return pl.pallas_call(
        functools.partial(_mlp_kernel, final_norm=final_norm),
        grid=(b, t // tm),
        in_specs=[
            pl.BlockSpec((1, tm, D), row),
            pl.BlockSpec((1, 1, 3 * D), lambda i, j: (mod0 + i, 0, 0)),
            _layer((1, D), layer),
            _layer((D, 3 * MLP_WIDTH), m, single_buffer=True),
            _layer((1, MLP_WIDTH), m),
            _layer((1, MLP_WIDTH), m),
            _layer((MLP_GROUPS, CHUNK, CHUNK), m),
            _layer((CHUNK, MLP_GROUPS), m),
            _layer((MLP_WIDTH, D), m, single_buffer=True),
            _full((1, D)),
        ],
        out_specs=pl.BlockSpec((1, tm, D), row),
        out_shape=jax.ShapeDtypeStruct((b, t, D), F32),
        scratch_shapes=[pltpu.VMEM((tm, MLP_WIDTH), F32), pltpu.VMEM((tm, MLP_WIDTH), F32)],
        compiler_params=_params(),
        name="mlp",
    )(x, mod, norm_g, w["w_in"], w["vg"], w["vb"], w["w_s"], w["b_s_t"], w["w_o"], final_g)


def _rope_tables(n_tokens):
    pos = np.arange(n_tokens)
    inv = 1.0 / (ROPE_THETA ** (np.arange(0, AXIS_ROPE, 2, dtype=np.float64) / AXIS_ROPE))
    ang_r, ang_c = (pos // GRID_W)[:, None] * inv, (pos % GRID_W)[:, None] * inv
    cr, sr, cc, sc = np.cos(ang_r), np.sin(ang_r), np.cos(ang_c), np.sin(ang_c)
    zero = np.zeros_like(sr)
    cos = np.concatenate([cr, cr, cc, cc], axis=1)
    sin_a = np.concatenate([-sr, zero, -sc, zero], axis=1)
    sin_b = np.concatenate([zero, sr, zero, sc], axis=1)
    return tuple(jnp.asarray(np.tile(t, (1, LANES // QK_ROPE)), F32) for t in (cos, sin_a, sin_b))


def _attn_weights(w_in, gq, gkv, w_uq, w_ukv, w_o):
    n = w_in.shape[0]
    kpe0 = Q_RANK + KV_RANK
    w_in_t = jnp.swapaxes(w_in, 1, 2)
    w_in_p = jnp.concatenate(
        [w_in_t[:, :kpe0 + QK_ROPE], w_in_t[:, kpe0:kpe0 + QK_ROPE], w_in_t[:, kpe0 + QK_ROPE:]], axis=1)
    ukv = w_ukv.reshape(n, KV_RANK, N_HEADS, QK_NOPE + V_DIM)
    w_uk = ukv[..., :QK_NOPE].reshape(n, KV_RANK, -1)
    w_uvt = jnp.swapaxes(ukv[..., QK_NOPE:].reshape(n, KV_RANK, -1), 1, 2)
    return dict(w_in=w_in_p.astype(BF16), gq=gq[:, None, :], gkv=gkv[:, None, :],
                w_uq=w_uq.astype(BF16), w_uk=w_uk.astype(BF16), w_uvt=w_uvt.astype(BF16),
                w_o=w_o.astype(BF16))


def _mlp_weights(w_in_bf16, vg, vb, w_s, b_s, w_o_bf16):
    return dict(w_in=w_in_bf16, vg=vg[:, None, :], vb=vb[:, None, :],
                w_s=w_s.astype(BF16), b_s_t=jnp.swapaxes(b_s, 1, 2), w_o=w_o_bf16)


def kernel(x_prompt, x_sample, cache_ckv, cache_kpe, c, c_ctx, norm_g, w_mod, b_mod, attn_w_in, attn_q_norm_g, attn_kv_norm_g, attn_w_uq, attn_w_ukv, attn_w_o, mlp_w_in, mlp_v_norm_g, mlp_v_norm_b, mlp_w_s, mlp_b_s, mlp_w_o, final_norm_g):
    depth = norm_g.shape[0]
    bc, tc, _ = x_prompt.shape
    bl, tl, _ = x_sample.shape
    if depth % 2:
        raise NotImplementedError("final RMSNorm is fused into a trailing gMLP layer")

    cond8 = jnp.concatenate([c_ctx[None, :], c, jnp.zeros((8 - 1 - bl, D), F32)], axis=0)
    mods = _modulation(cond8, w_mod, b_mod).reshape(depth * 8, 1, 3 * D)
    rope = _rope_tables(tl)
    norm_g = norm_g[:, None, :]
    final_g = final_norm_g.reshape(1, D)
    aw = _attn_weights(attn_w_in, attn_q_norm_g, attn_kv_norm_g, attn_w_uq, attn_w_ukv, attn_w_o)
    kc, vtc = _cache_kv(cache_ckv, jnp.swapaxes(cache_kpe, 2, 3), aw["w_uk"], aw["w_uvt"])

    xc, xl = x_prompt, x_sample
    cache_out = ()
    for layer in range(depth):
        mod0 = 8 * layer
        if layer % 2 == 0:
            a = layer // 2
            if layer == 0:
                xc, *cache_out, w_in_b, w_o_b = _ctx_attn_layer(
                    xc, mods, mod0, layer, a, norm_g, aw, cache_out, riders=(mlp_w_in, mlp_w_o))
                mw = _mlp_weights(w_in_b, mlp_v_norm_g, mlp_v_norm_b, mlp_w_s, mlp_b_s, w_o_b)
            else:
                xc, *cache_out = _ctx_attn_layer(xc, mods, mod0, layer, a, norm_g, aw, cache_out)
            q, kl, vtl, z = _lat_proj(xl, mods, mod0 + 1, layer, a, norm_g, rope, aw)
            xl = _lat_attn(xl, mods, mod0 + 1, a, q, z, kc, vtc, kl, vtl, aw["w_o"])
        else:
            m = layer // 2
            last = layer == depth - 1
            xc = _mlp_layer(xc.reshape(1, bc * tc, D), mods, mod0, layer, m, norm_g, mw, final_g,
                            last).reshape(bc, tc, D)
            xl = _mlp_layer(xl, mods, mod0 + 1, layer, m, norm_g, mw, final_g, last)
    new_ckv, new_kpe_t = cache_out
    return xc, xl, new_ckv, jnp.swapaxes(new_kpe_t, 2, 3)
```

```python
import functools
import math

import jax
import jax.numpy as jnp
import numpy as np
from jax import lax
from jax.experimental import pallas as pl
from jax.experimental.pallas import tpu as pltpu

F32 = jnp.float32
BF16 = jnp.bfloat16

D = 1024
N_HEADS = 8
Q_RANK = 512
KV_RANK = 256
QK_NOPE = 128
QK_ROPE = 64
V_DIM = 128
ATTN_WIDTH = N_HEADS * V_DIM
AXIS_ROPE = QK_ROPE // 2
ROPE_THETA = 10000.0
GRID_W = 64
CHUNK = 128
MLP_GROUPS = 8
MLP_WIDTH = 2 * D
GROUP_W = MLP_WIDTH // MLP_GROUPS
EPS = 1e-6
QK_SCALE = float((QK_NOPE + QK_ROPE) ** -0.5 * math.log2(math.e))

LANES = 128
ATTN_IN_PAD = Q_RANK + KV_RANK + LANES + ATTN_WIDTH
VMEM_LIMIT = 56 * 1024 * 1024

TOK_TILE = 512
Q_TILE = 256
CTX_ROWS = 2
QK_LEAD = 3
V_LEAD = 2
SPATIAL_LEAD = 2
KEY_CHUNK = 512


def _params():
    return pltpu.CompilerParams(vmem_limit_bytes=VMEM_LIMIT)


def _dot(a, b):
    return jnp.dot(a, b, preferred_element_type=F32)


def _dot_nt(a, b):
    return lax.dot_general(a, b, (((1,), (1,)), ((), ())), preferred_element_type=F32)


def _rms(x):
    return x * lax.rsqrt(jnp.mean(x * x, axis=-1, keepdims=True) + EPS)


def _silu(x):
    return x * jax.nn.sigmoid(x)


def _mod_parts(mod_ref):
    m = mod_ref[0]
    return m[:, :D], m[:, D:2 * D], m[:, 2 * D:]


def _norm_mod(x, g, shift, scale):
    return _rms(x) * (g * (1.0 + scale)) + shift


def _full(shape):
    n = len(shape)
    return pl.BlockSpec(shape, lambda *_: (0,) * n)


def _layer(shape, l, single_buffer=False):
    n = len(shape)
    mode = dict(pipeline_mode=pl.Buffered(1)) if single_buffer else {}
    return pl.BlockSpec((None,) + tuple(shape), lambda *_: (l,) + (0,) * n, **mode)


def _mod_kernel(cond_ref, w_ref, b_ref, o_ref):
    a = _silu(cond_ref[...]).astype(BF16)
    o_ref[0] = _dot(a, w_ref[0].astype(BF16)) + b_ref[0]


def _modulation(cond8, w_mod, b_mod):
    depth = w_mod.shape[0]
    tn = 1024
    return pl.pallas_call(
        _mod_kernel,
        grid=(depth, 3 * D // tn),
        in_specs=[
            pl.BlockSpec((8, D), lambda l, j: (0, 0)),
            pl.BlockSpec((1, D, tn), lambda l, j: (l, 0, j)),
            pl.BlockSpec((1, 1, tn), lambda l, j: (l, 0, j)),
        ],
        out_specs=pl.BlockSpec((1, 8, tn), lambda l, j: (l, 0, j)),
        out_shape=jax.ShapeDtypeStruct((depth, 8, 3 * D), F32),
        compiler_params=_params(),
        name="modulation",
    )(cond8, w_mod, b_mod.reshape(depth, 1, 3 * D))


def _rope(x, cos, sin_a, sin_b):
    return x * cos + pltpu.roll(x, LANES - 16, 1) * sin_a + pltpu.roll(x, 16, 1) * sin_b


def _kpe_tile(kpe2):
    lane = lax.broadcasted_iota(jnp.int32, kpe2.shape, 1)
    return jnp.where(lane < QK_ROPE, kpe2, jnp.zeros_like(kpe2))


def _q_heads(q, rope):
    lane = lax.broadcasted_iota(jnp.int32, (q.shape[0], LANES), 1)
    heads = []
    for pair in range(N_HEADS // 2):
        t0, t1, t2 = (q[:, (3 * pair + j) * LANES:(3 * pair + j + 1) * LANES] for j in range(3))
        nope_odd = jnp.where(lane < QK_ROPE, pltpu.roll(t1, QK_ROPE, 1), pltpu.roll(t2, QK_ROPE, 1))
        rot = jnp.where(lane < QK_ROPE, t1, t2)
        if rope is not None:
            rot = _rope(rot, *rope)
        heads += [(t0, rot), (nope_odd, pltpu.roll(rot, QK_ROPE, 1))]
    return heads


def _mla_project(h, w_in_ref, gq_ref, gkv_ref, w_uq_ref, w_uk_ref, w_uvt_ref, rope, defer_z=False):
    z_row0 = Q_RANK + KV_RANK + LANES
    proj = _dot_nt(h, w_in_ref[:z_row0 if defer_z else ATTN_IN_PAD, :])
    c_q = proj[:, :Q_RANK]
    c_kv = proj[:, Q_RANK:Q_RANK + KV_RANK]
    kpe2 = proj[:, Q_RANK + KV_RANK:z_row0]
    z = (lambda: _dot_nt(h, w_in_ref[z_row0:, :])) if defer_z else (lambda: proj[:, z_row0:])
    q = _dot((_rms(c_q) * gq_ref[...]).astype(BF16), w_uq_ref[...]) * QK_SCALE
    ckv = _rms(c_kv) * gkv_ref[...]
    k_nope = _dot(ckv.astype(BF16), w_uk_ref[...])
    v_t = _dot(w_uvt_ref[...], ckv.T.astype(BF16))
    kpe_rot = kpe2 if rope is None else _rope(kpe2, *rope)
    return _q_heads(q, rope), k_nope, kpe2, _kpe_tile(kpe_rot), v_t, ckv, z


def _gated_residual(x, o, z, gate, w_o_ref):
    mix = _dot((o * _silu(z)).astype(BF16), w_o_ref[...])
    return x + gate * mix


def _ctx_attn_kernel(*refs, n_prev, n_cast):
    (x_ref, mod_ref, g_ref, w_in_ref, gq_ref, gkv_ref, w_uq_ref, w_uk_ref, w_uvt_ref,
     w_o_ref) = refs[:10]
    refs = refs[10:]
    prev, refs = refs[:2 * bool(n_prev)], refs[2 * bool(n_prev):]
    cast_in, refs = refs[:n_cast], refs[n_cast:]
    (y_ref, ckv_ref, kpet_ref), refs = refs[:3], refs[3:]
    cast_out, (s_ref, p_ref) = refs[:n_cast], refs[n_cast:]
    for src, dst in zip(cast_in, cast_out):
        dst[...] = src[...].astype(BF16)
    shift, scale, gate = _mod_parts(mod_ref)
    if n_prev:
        ckv_ref[:, :n_prev] = prev[0][...]
        kpet_ref[:, :n_prev] = prev[1][...]
    for r in range(x_ref.shape[0]):
        x = x_ref[r]
        h = _norm_mod(x, g_ref[...], shift, scale).astype(BF16)
        q_heads, k_nope, kpe2, kpe_tile, v_t, ckv, z = _mla_project(
            h, w_in_ref, gq_ref, gkv_ref, w_uq_ref, w_uk_ref, w_uvt_ref, None, defer_z=True)
        ckv_ref[r, n_prev] = ckv
        kpet_ref[r, n_prev] = kpe2.T[:QK_ROPE]
        kpe_tile = kpe_tile.astype(BF16)
        k_nope = k_nope.astype(BF16)
        v_t = v_t.astype(BF16)
        for hd in range(N_HEADS):
            q_h = jnp.concatenate([t.astype(BF16) for t in q_heads[hd]], axis=1)
            k_h = jnp.concatenate([k_nope[:, hd * LANES:(hd + 1) * LANES], kpe_tile], axis=1)
            s_ref[r, hd] = _dot_nt(k_h, q_h)
        z = z()
        inv = []
        for hd in range(N_HEADS):
            s = s_ref[r, hd]
            p = jnp.exp2(s - s.max(axis=0, keepdims=True))
            inv.append(1.0 / p.sum(axis=0, keepdims=True))
            p_ref[r, hd] = p.astype(BF16)
        o_t = [_dot(v_t[hd * LANES:(hd + 1) * LANES, :], p_ref[r, hd]) * inv[hd]
               for hd in range(N_HEADS)]
        o = jnp.concatenate([t.T for t in o_t], axis=1)
        y_ref[r] = _gated_residual(x, o, z, gate, w_o_ref)


def _ctx_attn_layer(x, mod, mod0, layer, a, norm_g, w, prev, riders=()):
    b, t, _ = x.shape
    rows = CTX_ROWS
    steps = b // rows
    blk = lambda i: (i, 0, 0, 0)
    chunk = lambda p: pl.BlockSpec((p.shape[0], p.shape[1] // steps, p.shape[2]), lambda i: (0, i, 0))
    return pl.pallas_call(
        functools.partial(_ctx_attn_kernel, n_prev=a, n_cast=len(riders)),
        grid=(steps,),
        in_specs=[
            pl.BlockSpec((rows, t, D), lambda i: (i, 0, 0)),
            pl.BlockSpec((1, 1, 3 * D), lambda i: (mod0, 0, 0)),
            _layer((1, D), layer),
            _layer((ATTN_IN_PAD, D), a),
            _layer((1, Q_RANK), a),
            _layer((1, KV_RANK), a),
            _layer((Q_RANK, N_HEADS * (QK_NOPE + QK_ROPE)), a),
            _layer((KV_RANK, N_HEADS * QK_NOPE), a),
            _layer((N_HEADS * V_DIM, KV_RANK), a),
            _layer((ATTN_WIDTH, D), a),
        ] + ([pl.BlockSpec((rows, a, t, KV_RANK), blk), pl.BlockSpec((rows, a, QK_ROPE, t), blk)]
             if a else []) + [chunk(p) for p in riders],
        out_specs=[
            pl.BlockSpec((rows, t, D), lambda i: (i, 0, 0)),
            pl.BlockSpec((rows, a + 1, t, KV_RANK), blk),
            pl.BlockSpec((rows, a + 1, QK_ROPE, t), blk),
        ] + [chunk(p) for p in riders],
        out_shape=[
            jax.ShapeDtypeStruct((b, t, D), F32),
            jax.ShapeDtypeStruct((b, a + 1, t, KV_RANK), F32),
            jax.ShapeDtypeStruct((b, a + 1, QK_ROPE, t), F32),
        ] + [jax.ShapeDtypeStruct(p.shape, BF16) for p in riders],
        scratch_shapes=[pltpu.VMEM((rows, N_HEADS, t, t), F32), pltpu.VMEM((rows, N_HEADS, t, t), BF16)],
        compiler_params=_params(),
        name="ctx_attn",
    )(x, mod, norm_g, w["w_in"], w["gq"], w["gkv"], w["w_uq"], w["w_uk"], w["w_uvt"], w["w_o"], *prev,
      *riders)


def _store_heads(ref, lead, tiles):
    for hd, (nope, rot) in enumerate(tiles):
        ref[lead + (hd, slice(None), slice(0, LANES))] = nope.astype(BF16)
        ref[lead + (hd, slice(None), slice(LANES, 2 * LANES))] = rot.astype(BF16)


def _key_tiles(k_nope, kpe_tile):
    return [(k_nope[:, hd * LANES:(hd + 1) * LANES], kpe_tile) for hd in range(N_HEADS)]


def _cache_kv_kernel(ckv_ref, kpet_ref, w_uk_ref, w_uvt_ref, k_ref, vt_ref):
    ckv = ckv_ref[0, 0]
    k_nope = _dot(ckv.astype(BF16), w_uk_ref[0])
    kpe_t = kpet_ref[0, 0]
    kpe2 = jnp.concatenate([kpe_t] * (LANES // QK_ROPE), axis=0).T
    _store_heads(k_ref, (0, 0), _key_tiles(k_nope, _kpe_tile(kpe2)))
    v_t = _dot(w_uvt_ref[0], ckv.T.astype(BF16)).astype(BF16)
    for hd in range(N_HEADS):
        vt_ref[0, 0, hd] = v_t[hd * V_DIM:(hd + 1) * V_DIM, :]


def _cache_kv(cache_ckv, cache_kpe_t, w_uk, w_uvt):
    b, n_attn, past, _ = cache_ckv.shape
    hk = N_HEADS * QK_NOPE
    return pl.pallas_call(
        _cache_kv_kernel,
        grid=(n_attn, b),
        in_specs=[
            pl.BlockSpec((1, 1, past, KV_RANK), lambda a, i: (i, a, 0, 0)),
            pl.BlockSpec((1, 1, QK_ROPE, past), lambda a, i: (i, a, 0, 0)),
            pl.BlockSpec((1, KV_RANK, hk), lambda a, i: (a, 0, 0)),
            pl.BlockSpec((1, ATTN_WIDTH, KV_RANK), lambda a, i: (a, 0, 0)),
        ],
        out_specs=[
            pl.BlockSpec((1, 1, N_HEADS, past, 2 * LANES), lambda a, i: (a, i, 0, 0, 0)),
            pl.BlockSpec((1, 1, N_HEADS, V_DIM, past), lambda a, i: (a, i, 0, 0, 0)),
        ],
        out_shape=[
            jax.ShapeDtypeStruct((n_attn, b, N_HEADS, past, 2 * LANES), BF16),
            jax.ShapeDtypeStruct((n_attn, b, N_HEADS, V_DIM, past), BF16),
        ],
        compiler_params=_params(),
        name="cache_kv",
    )(cache_ckv, cache_kpe_t, w_uk, w_uvt)


def _lat_proj_kernel(x_ref, mod_ref, g_ref, cos_ref, sa_ref, sb_ref, w_in_ref, gq_ref, gkv_ref,
                     w_uq_ref, w_uk_ref, w_uvt_ref, q_ref, k_ref, vt_ref, z_ref):
    shift, scale, _ = _mod_parts(mod_ref)
    h = _norm_mod(x_ref[0], g_ref[...], shift, scale).astype(BF16)
    rope = (cos_ref[...], sa_ref[...], sb_ref[...])
    q_heads, k_nope, _, kpe_tile, v_t, _, z = _mla_project(
        h, w_in_ref, gq_ref, gkv_ref, w_uq_ref, w_uk_ref, w_uvt_ref, rope)
    _store_heads(q_ref, (0,), q_heads)
    _store_heads(k_ref, (0,), _key_tiles(k_nope, kpe_tile))
    v_t = v_t.astype(BF16)
    for hd in range(N_HEADS):
        vt_ref[0, hd] = v_t[hd * V_DIM:(hd + 1) * V_DIM, :]
    z_ref[0] = z()


def _lat_proj(x, mod, mod0, layer, a, norm_g, rope, w):
    b, t, _ = x.shape
    tm = TOK_TILE
    hk = N_HEADS * QK_NOPE
    hr = N_HEADS * QK_ROPE
    row = lambda i, j: (i, j, 0)
    head_row = lambda i, j: (i, 0, j, 0)
    tab = pl.BlockSpec((tm, LANES), lambda i, j: (j, 0))
    return pl.pallas_call(
        _lat_proj_kernel,
        grid=(b, t // tm),
        in_specs=[
            pl.BlockSpec((1, tm, D), row),
            pl.BlockSpec((1, 1, 3 * D), lambda i, j: (mod0 + i, 0, 0)),
            _layer((1, D), layer),
            tab, tab, tab,
            _layer((ATTN_IN_PAD, D), a),
            _layer((1, Q_RANK), a),
            _layer((1, KV_RANK), a),
            _layer((Q_RANK, hk + hr), a),
            _layer((KV_RANK, hk), a),
            _layer((ATTN_WIDTH, KV_RANK), a),
        ],
        out_specs=[
            pl.BlockSpec((1, N_HEADS, tm, 2 * LANES), head_row),
            pl.BlockSpec((1, N_HEADS, tm, 2 * LANES), head_row),
            pl.BlockSpec((1, N_HEADS, V_DIM, tm), lambda i, j: (i, 0, 0, j)),
            pl.BlockSpec((1, tm, ATTN_WIDTH), row),
        ],
        out_shape=[
            jax.ShapeDtypeStruct((b, N_HEADS, t, 2 * LANES), BF16),
            jax.ShapeDtypeStruct((b, N_HEADS, t, 2 * LANES), BF16),
            jax.ShapeDtypeStruct((b, N_HEADS, V_DIM, t), BF16),
            jax.ShapeDtypeStruct((b, t, ATTN_WIDTH), F32),
        ],
        compiler_params=_params(),
        name="lat_proj",
    )(x, mod, norm_g, *rope, w["w_in"], w["gq"], w["gkv"], w["w_uq"], w["w_uk"], w["w_uvt"])


def _head_stage(hd_pv, hd_qk, q_ref, kc_ref, kl_ref, vtc_ref, vtl_ref, ot_ref, cur, nxt):
    past = kc_ref.shape[3]
    n_keys = past + kl_ref.shape[2]
    tq = q_ref.shape[2]
    chunks = list(range(0, n_keys, KEY_CHUNK))

    def k_of(hd, c0):
        if c0 < past:
            return kc_ref[0, 0, hd, c0:c0 + KEY_CHUNK]
        return kl_ref[0, hd, c0 - past:c0 - past + KEY_CHUNK]

    def vt_of(hd, c0):
        if c0 < past:
            return vtc_ref[0, 0, hd, :, c0:c0 + KEY_CHUNK]
        return vtl_ref[0, hd, :, c0 - past:c0 - past + KEY_CHUNK]

    col_max = []

    def scores(c0):
        s = _dot_nt(k_of(hd_qk, c0), q_h)
        nxt[0][c0:c0 + KEY_CHUNK] = s
        col_max.append(s.max(axis=0, keepdims=True))

    if hd_qk is not None:
        q_h = q_ref[0, hd_qk]
        for c0 in chunks[:QK_LEAD]:
            scores(c0)
    if hd_pv is not None:
        m = cur[1][...]
        acc = jnp.zeros((V_DIM, tq), F32)
        denom = jnp.zeros((1, tq), F32)
    for i, c0 in enumerate(chunks):
        if hd_pv is not None:
            p = jnp.exp2(cur[0][c0:c0 + KEY_CHUNK] - m)
            denom = denom + p.sum(axis=0, keepdims=True)
            acc = acc + _dot(vt_of(hd_pv, c0), p.astype(BF16))
        if hd_qk is not None and i + QK_LEAD < len(chunks):
            scores(chunks[i + QK_LEAD])
    if hd_qk is not None:
        nxt[1][...] = functools.reduce(jnp.maximum, col_max)
    if hd_pv is not None:
        ot_ref[hd_pv] = acc * (1.0 / denom)


def _lat_attn_kernel(x_ref, mod_ref, q_ref, z_ref, kc_ref, vtc_ref, kl_ref, vtl_ref, w_o_ref,
                     y_ref, s0_ref, s1_ref, m0_ref, m1_ref, ot_ref):
    _, _, gate = _mod_parts(mod_ref)
    stage = functools.partial(_head_stage, q_ref=q_ref, kc_ref=kc_ref, kl_ref=kl_ref,
                              vtc_ref=vtc_ref, vtl_ref=vtl_ref, ot_ref=ot_ref)
    bufs = ((s0_ref, m0_ref), (s1_ref, m1_ref))

    for hd in range(-1, N_HEADS):
        stage(hd if hd >= 0 else None, hd + 1 if hd + 1 < N_HEADS else None,
              cur=bufs[hd % 2], nxt=bufs[(hd + 1) % 2])

    o = jnp.concatenate([ot_ref[hd].T for hd in range(N_HEADS)], axis=1)
    y_ref[0] = _gated_residual(x_ref[0], o, z_ref[0], gate, w_o_ref)


def _lat_attn(x, mod, mod0, a, q, z, kc, vtc, kl, vtl, w_o):
    b, t, _ = x.shape
    tq = Q_TILE
    past = kc.shape[3]
    row = lambda i, j: (i, j, 0)
    per_b = lambda i, j: (i, 0, 0, 0)
    cache = lambda i, j: (a, i, 0, 0, 0)
    return pl.pallas_call(
        _lat_attn_kernel,
        grid=(b, t // tq),
        in_specs=[
            pl.BlockSpec((1, tq, D), row),
            pl.BlockSpec((1, 1, 3 * D), lambda i, j: (mod0 + i, 0, 0)),
            pl.BlockSpec((1, N_HEADS, tq, 2 * LANES), lambda i, j: (i, 0, j, 0)),
            pl.BlockSpec((1, tq, ATTN_WIDTH), row),
            pl.BlockSpec((1, 1, N_HEADS, past, 2 * LANES), cache),
            pl.BlockSpec((1, 1, N_HEADS, V_DIM, past), cache),
            pl.BlockSpec((1, N_HEADS, t, 2 * LANES), per_b),
            pl.BlockSpec((1, N_HEADS, V_DIM, t), per_b),
            _layer((ATTN_WIDTH, D), a),
        ],
        out_specs=pl.BlockSpec((1, tq, D), row),
        out_shape=jax.ShapeDtypeStruct((b, t, D), F32),
        scratch_shapes=[
            pltpu.VMEM((past + t, tq), F32), pltpu.VMEM((past + t, tq), F32),
            pltpu.VMEM((1, tq), F32), pltpu.VMEM((1, tq), F32),
            pltpu.VMEM((N_HEADS, V_DIM, tq), F32),
        ],
        compiler_params=_params(),
        name="lat_attn",
    )(x, mod, q, z, kc, vtc, kl, vtl, w_o)


def _mlp_kernel(x_ref, mod_ref, g_ref, w_in_ref, vg_ref, vb_ref, w_s_ref, b_s_ref, w_o_ref,
                fg_ref, y_ref, v_ref, uz_ref, *, final_norm):
    x = x_ref[0]
    tm = x.shape[0]
    shift, scale, gate = _mod_parts(mod_ref)
    h = _norm_mod(x, g_ref[...], shift, scale).astype(BF16)

    s1 = [jnp.zeros((tm, LANES), F32)]
    s2 = [jnp.zeros((tm, LANES), F32)]
    shift_c = []

    def v_group(g):
        vcols = slice(MLP_WIDTH + g * GROUP_W, MLP_WIDTH + (g + 1) * GROUP_W)
        v = jax.nn.gelu(_dot(h, w_in_ref[:, vcols]))
        v_ref[:, g * GROUP_W:(g + 1) * GROUP_W] = v
        tiles = [v[:, j * LANES:(j + 1) * LANES] for j in range(GROUP_W // LANES)]
        if g == 0:
            shift_c.append(jnp.sum(sum(tiles[1:], tiles[0]), axis=-1, keepdims=True) * (1.0 / GROUP_W))
        for t in tiles:
            d = t - shift_c[0]
            s1[0] = s1[0] + d
            s2[0] = s2[0] + d * d

    def uz_group(g):
        cols = slice(g * GROUP_W, (g + 1) * GROUP_W)
        zcols = slice(2 * MLP_WIDTH + g * GROUP_W, 2 * MLP_WIDTH + (g + 1) * GROUP_W)
        u = jax.nn.gelu(_dot(h, w_in_ref[:, cols]))
        z = _dot(h, w_in_ref[:, zcols])
        uz_ref[:, cols] = u * _silu(z)

    for g in range(V_LEAD):
        v_group(g)
    for g in range(MLP_GROUPS):
        if g + V_LEAD < MLP_GROUPS:
            v_group(g + V_LEAD)
        if g == MLP_GROUPS - V_LEAD:
            m1 = jnp.sum(s1[0], axis=-1, keepdims=True) * (1.0 / MLP_WIDTH)
            mu = shift_c[0] + m1
            var = jnp.sum(s2[0], axis=-1, keepdims=True) * (1.0 / MLP_WIDTH) - m1 * m1
            rstd = lax.rsqrt(var + EPS)
        uz_group(g)

    def spatial(g):
        cols = slice(g * GROUP_W, (g + 1) * GROUP_W)
        vn = ((v_ref[:, cols] - mu) * rstd * vg_ref[:, cols] + vb_ref[:, cols]).astype(BF16)
        bias = b_s_ref[:, g:g + 1]
        return jnp.concatenate(
            [_dot(w_s_ref[g], vn[c * CHUNK:(c + 1) * CHUNK]) + bias for c in range(tm // CHUNK)],
            axis=0)

    gates = [spatial(g) for g in range(SPATIAL_LEAD)]
    acc = None
    for g in range(MLP_GROUPS):
        cols = slice(g * GROUP_W, (g + 1) * GROUP_W)
        part = _dot((uz_ref[:, cols] * gates[g]).astype(BF16), w_o_ref[cols, :])
        acc = part if acc is None else acc + part
        if g + SPATIAL_LEAD < MLP_GROUPS:
            gates.append(spatial(g + SPATIAL_LEAD))

    y = x + gate * acc
    if final_norm:
        y = _rms(y) * fg_ref[...]
    y_ref[0] = y


def _mlp_layer(x, mod, mod0, layer, m, norm_g, w, final_g, final_norm):
    b, t, _ = x.shape
    tm = TOK_TILE
    row = lambda i, j: (i, j, 0)
    return pl.pallas_call(
        functools.partial(_mlp_kernel, final_norm=final_norm),
        grid=(b, t // tm),
        in_specs=[
            pl.BlockSpec((1, tm, D), row),
            pl.BlockSpec((1, 1, 3 * D), lambda i, j: (mod0 + i, 0, 0)),
            _layer((1, D), layer),
            _layer((D, 3 * MLP_WIDTH), m, single_buffer=True),
            _layer((1, MLP_WIDTH), m),
            _layer((1, MLP_WIDTH), m),
            _layer((MLP_GROUPS, CHUNK, CHUNK), m),
            _layer((CHUNK, MLP_GROUPS), m),
            _layer((MLP_WIDTH, D), m, single_buffer=True),
            _full((1, D)),
        ],
        out_specs=pl.BlockSpec((1, tm, D), row),
        out_shape=jax.ShapeDtypeStruct((b, t, D), F32),
        scratch_shapes=[pltpu.VMEM((tm, MLP_WIDTH), F32), pltpu.VMEM((tm, MLP_WIDTH), F32)],
        compiler_params=_params(),
        name="mlp",
    )(x, mod, norm_g, w["w_in"], w["vg"], w["vb"], w["w_s"], w["b_s_t"], w["w_o"], final_g)


def _rope_tables(n_tokens):
    pos = np.arange(n_tokens)
    inv = 1.0 / (ROPE_THETA ** (np.arange(0, AXIS_ROPE, 2, dtype=np.float64) / AXIS_ROPE))
    ang_r, ang_c = (pos // GRID_W)[:, None] * inv, (pos % GRID_W)[:, None] * inv
    cr, sr, cc, sc = np.cos(ang_r), np.sin(ang_r), np.cos(ang_c), np.sin(ang_c)
    zero = np.zeros_like(sr)
    cos = np.concatenate([cr, cr, cc, cc], axis=1)
    sin_a = np.concatenate([-sr, zero, -sc, zero], axis=1)
    sin_b = np.concatenate([zero, sr, zero, sc], axis=1)
    return tuple(jnp.asarray(np.tile(t, (1, LANES // QK_ROPE)), F32) for t in (cos, sin_a, sin_b))


def _attn_weights(w_in, gq, gkv, w_uq, w_ukv, w_o):
    n = w_in.shape[0]
    kpe0 = Q_RANK + KV_RANK
    w_in_t = jnp.swapaxes(w_in, 1, 2)
    w_in_p = jnp.concatenate(
        [w_in_t[:, :kpe0 + QK_ROPE], w_in_t[:, kpe0:kpe0 + QK_ROPE], w_in_t[:, kpe0 + QK_ROPE:]], axis=1)
    ukv = w_ukv.reshape(n, KV_RANK, N_HEADS, QK_NOPE + V_DIM)
    w_uk = ukv[..., :QK_NOPE].reshape(n, KV_RANK, -1)
    w_uvt = jnp.swapaxes(ukv[..., QK_NOPE:].reshape(n, KV_RANK, -1), 1, 2)
    return dict(w_in=w_in_p.astype(BF16), gq=gq[:, None, :], gkv=gkv[:, None, :],
                w_uq=w_uq.astype(BF16), w_uk=w_uk.astype(BF16), w_uvt=w_uvt.astype(BF16),
                w_o=w_o.astype(BF16))


def _mlp_weights(w_in_bf16, vg, vb, w_s, b_s, w_o_bf16):
    return dict(w_in=w_in_bf16, vg=vg[:, None, :], vb=vb[:, None, :],
                w_s=w_s.astype(BF16), b_s_t=jnp.swapaxes(b_s, 1, 2), w_o=w_o_bf16)


def kernel(x_prompt, x_sample, cache_ckv, cache_kpe, c, c_ctx, norm_g, w_mod, b_mod, attn_w_in, attn_q_norm_g, attn_kv_norm_g, attn_w_uq, attn_w_ukv, attn_w_o, mlp_w_in, mlp_v_norm_g, mlp_v_norm_b, mlp_w_s, mlp_b_s, mlp_w_o, final_norm_g):
    depth = norm_g.shape[0]
    bc, tc, _ = x_prompt.shape
    bl, tl, _ = x_sample.shape
    if depth % 2:
        raise NotImplementedError("final RMSNorm is fused into a trailing gMLP layer")

    cond8 = jnp.concatenate([c_ctx[None, :], c, jnp.zeros((8 - 1 - bl, D), F32)], axis=0)
    mods = _modulation(cond8, w_mod, b_mod).reshape(depth * 8, 1, 3 * D)
    rope = _rope_tables(tl)
    norm_g = norm_g[:, None, :]
    final_g = final_norm_g.reshape(1, D)
    aw = _attn_weights(attn_w_in, attn_q_norm_g, attn_kv_norm_g, attn_w_uq, attn_w_ukv, attn_w_o)
    kc, vtc = _cache_kv(cache_ckv, jnp.swapaxes(cache_kpe, 2, 3), aw["w_uk"], aw["w_uvt"])

    xc, xl = x_prompt, x_sample
    cache_out = ()
    for layer in range(depth):
        mod0 = 8 * layer
        if layer % 2 == 0:
            a = layer // 2
            if layer == 0:
                xc, *cache_out, w_in_b, w_o_b = _ctx_attn_layer(
                    xc, mods, mod0, layer, a, norm_g, aw, cache_out, riders=(mlp_w_in, mlp_w_o))
                mw = _mlp_weights(w_in_b, mlp_v_norm_g, mlp_v_norm_b, mlp_w_s, mlp_b_s, w_o_b)
            else:
                xc, *cache_out = _ctx_attn_layer(xc, mods, mod0, layer, a, norm_g, aw, cache_out)
            q, kl, vtl, z = _lat_proj(xl, mods, mod0 + 1, layer, a, norm_g, rope, aw)
            xl = _lat_attn(xl, mods, mod0 + 1, a, q, z, kc, vtc, kl, vtl, aw["w_o"])
        else:
            m = layer // 2
            last = layer == depth - 1
            xc = _mlp_layer(xc.reshape(1, bc * tc, D), mods, mod0, layer, m, norm_g, mw, final_g,
                            last).reshape(bc, tc, D)
            xl = _mlp_layer(xl, mods, mod0 + 1, layer, m, norm_g, mw, final_g, last)
    new_ckv, new_kpe_t = cache_out
    return xc, xl, new_ckv, jnp.swapaxes(new_kpe_t, 2, 3)
```

```python
import functools
import math

import jax
import jax.numpy as jnp
import numpy as np
from jax import lax
from jax.experimental import pallas as pl
from jax.experimental.pallas import tpu as pltpu

F32 = jnp.float32
BF16 = jnp.bfloat16

D = 1024
N_HEADS = 8
Q_RANK = 512
KV_RANK = 256
QK_NOPE = 128
QK_ROPE = 64
V_DIM = 128
ATTN_WIDTH = N_HEADS * V_DIM
AXIS_ROPE = QK_ROPE // 2
ROPE_THETA = 10000.0
GRID_W = 64
CHUNK = 128
MLP_GROUPS = 8
MLP_WIDTH = 2 * D
GROUP_W = MLP_WIDTH // MLP_GROUPS
EPS = 1e-6
QK_SCALE = float((QK_NOPE + QK_ROPE) ** -0.5 * math.log2(math.e))

LANES = 128
ATTN_IN_PAD = Q_RANK + KV_RANK + LANES + ATTN_WIDTH
VMEM_LIMIT = 56 * 1024 * 1024

TOK_TILE = 512
Q_TILE = 256
CTX_ROWS = 2
QK_LEAD = 3
V_LEAD = 1
SPATIAL_LEAD = 2
KEY_CHUNK = 512


def _params():
    return pltpu.CompilerParams(vmem_limit_bytes=VMEM_LIMIT)


def _dot(a, b):
    return jnp.dot(a, b, preferred_element_type=F32)


def _dot_nt(a, b):
    return lax.dot_general(a, b, (((1,), (1,)), ((), ())), preferred_element_type=F32)


def _rms(x):
    return x * lax.rsqrt(jnp.mean(x * x, axis=-1, keepdims=True) + EPS)


def _silu(x):
    return x * jax.nn.sigmoid(x)


def _mod_parts(mod_ref):
    m = mod_ref[0]
    return m[:, :D], m[:, D:2 * D], m[:, 2 * D:]


def _norm_mod(x, g, shift, scale):
    return _rms(x) * (g * (1.0 + scale)) + shift


def _full(shape):
    n = len(shape)
    return pl.BlockSpec(shape, lambda *_: (0,) * n)


def _layer(shape, l, single_buffer=False):
    n = len(shape)
    mode = dict(pipeline_mode=pl.Buffered(1)) if single_buffer else {}
    return pl.BlockSpec((None,) + tuple(shape), lambda *_: (l,) + (0,) * n, **mode)


def _mod_kernel(cond_ref, w_ref, b_ref, o_ref):
    a = _silu(cond_ref[...]).astype(BF16)
    o_ref[0] = _dot(a, w_ref[0].astype(BF16)) + b_ref[0]


def _modulation(cond8, w_mod, b_mod):
    depth = w_mod.shape[0]
    tn = 1024
    return pl.pallas_call(
        _mod_kernel,
        grid=(depth, 3 * D // tn),
        in_specs=[
            pl.BlockSpec((8, D), lambda l, j: (0, 0)),
            pl.BlockSpec((1, D, tn), lambda l, j: (l, 0, j)),
            pl.BlockSpec((1, 1, tn), lambda l, j: (l, 0, j)),
        ],
        out_specs=pl.BlockSpec((1, 8, tn), lambda l, j: (l, 0, j)),
        out_shape=jax.ShapeDtypeStruct((depth, 8, 3 * D), F32),
        compiler_params=_params(),
        name="modulation",
    )(cond8, w_mod, b_mod.reshape(depth, 1, 3 * D))


def _rope(x, cos, sin_a, sin_b):
    return x * cos + pltpu.roll(x, LANES - 16, 1) * sin_a + pltpu.roll(x, 16, 1) * sin_b


def _kpe_tile(kpe2):
    lane = lax.broadcasted_iota(jnp.int32, kpe2.shape, 1)
    return jnp.where(lane < QK_ROPE, kpe2, jnp.zeros_like(kpe2))


def _q_heads(q, rope):
    lane = lax.broadcasted_iota(jnp.int32, (q.shape[0], LANES), 1)
    heads = []
    for pair in range(N_HEADS // 2):
        t0, t1, t2 = (q[:, (3 * pair + j) * LANES:(3 * pair + j + 1) * LANES] for j in range(3))
        nope_odd = jnp.where(lane < QK_ROPE, pltpu.roll(t1, QK_ROPE, 1), pltpu.roll(t2, QK_ROPE, 1))
        rot = jnp.where(lane < QK_ROPE, t1, t2)
        if rope is not None:
            rot = _rope(rot, *rope)
        heads += [(t0, rot), (nope_odd, pltpu.roll(rot, QK_ROPE, 1))]
    return heads


def _mla_project(h, w_in_ref, gq_ref, gkv_ref, w_uq_ref, w_uk_ref, w_uvt_ref, rope, defer_z=False):
    z_row0 = Q_RANK + KV_RANK + LANES
    proj = _dot_nt(h, w_in_ref[:z_row0 if defer_z else ATTN_IN_PAD, :])
    c_q = proj[:, :Q_RANK]
    c_kv = proj[:, Q_RANK:Q_RANK + KV_RANK]
    kpe2 = proj[:, Q_RANK + KV_RANK:z_row0]
    z = (lambda: _dot_nt(h, w_in_ref[z_row0:, :])) if defer_z else (lambda: proj[:, z_row0:])
    q = _dot((_rms(c_q) * gq_ref[...]).astype(BF16), w_uq_ref[...]) * QK_SCALE
    ckv = _rms(c_kv) * gkv_ref[...]
    k_nope = _dot(ckv.astype(BF16), w_uk_ref[...])
    v_t = _dot(w_uvt_ref[...], ckv.T.astype(BF16))
    kpe_rot = kpe2 if rope is None else _rope(kpe2, *rope)
    return _q_heads(q, rope), k_nope, kpe2, _kpe_tile(kpe_rot), v_t, ckv, z


def _gated_residual(x, o, z, gate, w_o_ref):
    mix = _dot((o * _silu(z)).astype(BF16), w_o_ref[...])
    return x + gate * mix


def _ctx_attn_kernel(*refs, n_prev, n_cast):
    (x_ref, mod_ref, g_ref, w_in_ref, gq_ref, gkv_ref, w_uq_ref, w_uk_ref, w_uvt_ref,
     w_o_ref) = refs[:10]
    refs = refs[10:]
    prev, refs = refs[:2 * bool(n_prev)], refs[2 * bool(n_prev):]
    cast_in, refs = refs[:n_cast], refs[n_cast:]
    (y_ref, ckv_ref, kpet_ref), refs = refs[:3], refs[3:]
    cast_out, (s_ref, p_ref) = refs[:n_cast], refs[n_cast:]
    for src, dst in zip(cast_in, cast_out):
        dst[...] = src[...].astype(BF16)
    shift, scale, gate = _mod_parts(mod_ref)
    if n_prev:
        ckv_ref[:, :n_prev] = prev[0][...]
        kpet_ref[:, :n_prev] = prev[1][...]
    for r in range(x_ref.shape[0]):
        x = x_ref[r]
        h = _norm_mod(x, g_ref[...], shift, scale).astype(BF16)
        q_heads, k_nope, kpe2, kpe_tile, v_t, ckv, z = _mla_project(
            h, w_in_ref, gq_ref, gkv_ref, w_uq_ref, w_uk_ref, w_uvt_ref, None, defer_z=True)
        ckv_ref[r, n_prev] = ckv
        kpet_ref[r, n_prev] = kpe2.T[:QK_ROPE]
        kpe_tile = kpe_tile.astype(BF16)
        k_nope = k_nope.astype(BF16)
        v_t = v_t.astype(BF16)
        for hd in range(N_HEADS):
            q_h = jnp.concatenate([t.astype(BF16) for t in q_heads[hd]], axis=1)
            k_h = jnp.concatenate([k_nope[:, hd * LANES:(hd + 1) * LANES], kpe_tile], axis=1)
            s_ref[r, hd] = _dot_nt(k_h, q_h)
        z = z()
        inv = []
        for hd in range(N_HEADS):
            s = s_ref[r, hd]
            p = jnp.exp2(s - s.max(axis=0, keepdims=True))
            inv.append(1.0 / p.sum(axis=0, keepdims=True))
            p_ref[r, hd] = p.astype(BF16)
        o_t = [_dot(v_t[hd * LANES:(hd + 1) * LANES, :], p_ref[r, hd]) * inv[hd]
               for hd in range(N_HEADS)]
        o = jnp.concatenate([t.T for t in o_t], axis=1)
        y_ref[r] = _gated_residual(x, o, z, gate, w_o_ref)


def _ctx_attn_layer(x, mod, mod0, layer, a, norm_g, w, prev, riders=()):
    b, t, _ = x.shape
    rows = CTX_ROWS
    steps = b // rows
    blk = lambda i: (i, 0, 0, 0)
    chunk = lambda p: pl.BlockSpec((p.shape[0], p.shape[1] // steps, p.shape[2]), lambda i: (0, i, 0))
    return pl.pallas_call(
        functools.partial(_ctx_attn_kernel, n_prev=a, n_cast=len(riders)),
        grid=(steps,),
        in_specs=[
            pl.BlockSpec((rows, t, D), lambda i: (i, 0, 0)),
            pl.BlockSpec((1, 1, 3 * D), lambda i: (mod0, 0, 0)),
            _layer((1, D), layer),
            _layer((ATTN_IN_PAD, D), a),
            _layer((1, Q_RANK), a),
            _layer((1, KV_RANK), a),
            _layer((Q_RANK, N_HEADS * (QK_NOPE + QK_ROPE)), a),
            _layer((KV_RANK, N_HEADS * QK_NOPE), a),
            _layer((N_HEADS * V_DIM, KV_RANK), a),
            _layer((ATTN_WIDTH, D), a),
        ] + ([pl.BlockSpec((rows, a, t, KV_RANK), blk), pl.BlockSpec((rows, a, QK_ROPE, t), blk)]
             if a else []) + [chunk(p) for p in riders],
        out_specs=[
            pl.BlockSpec((rows, t, D), lambda i: (i, 0, 0)),
            pl.BlockSpec((rows, a + 1, t, KV_RANK), blk),
            pl.BlockSpec((rows, a + 1, QK_ROPE, t), blk),
        ] + [chunk(p) for p in riders],
        out_shape=[
            jax.ShapeDtypeStruct((b, t, D), F32),
            jax.ShapeDtypeStruct((b, a + 1, t, KV_RANK), F32),
            jax.ShapeDtypeStruct((b, a + 1, QK_ROPE, t), F32),
        ] + [jax.ShapeDtypeStruct(p.shape, BF16) for p in riders],
        scratch_shapes=[pltpu.VMEM((rows, N_HEADS, t, t), F32), pltpu.VMEM((rows, N_HEADS, t, t), BF16)],
        compiler_params=_params(),
        name="ctx_attn",
    )(x, mod, norm_g, w["w_in"], w["gq"], w["gkv"], w["w_uq"], w["w_uk"], w["w_uvt"], w["w_o"], *prev,
      *riders)


def _store_heads(ref, lead, tiles):
    for hd, (nope, rot) in enumerate(tiles):
        ref[lead + (hd, slice(None), slice(0, LANES))] = nope.astype(BF16)
        ref[lead + (hd, slice(None), slice(LANES, 2 * LANES))] = rot.astype(BF16)


def _key_tiles(k_nope, kpe_tile):
    return [(k_nope[:, hd * LANES:(hd + 1) * LANES], kpe_tile) for hd in range(N_HEADS)]


def _cache_kv_kernel(ckv_ref, kpet_ref, w_uk_ref, w_uvt_ref, k_ref, vt_ref):
    ckv = ckv_ref[0, 0]
    k_nope = _dot(ckv.astype(BF16), w_uk_ref[0])
    kpe_t = kpet_ref[0, 0]
    kpe2 = jnp.concatenate([kpe_t] * (LANES // QK_ROPE), axis=0).T
    _store_heads(k_ref, (0, 0), _key_tiles(k_nope, _kpe_tile(kpe2)))
    v_t = _dot(w_uvt_ref[0], ckv.T.astype(BF16)).astype(BF16)
    for hd in range(N_HEADS):
        vt_ref[0, 0, hd] = v_t[hd * V_DIM:(hd + 1) * V_DIM, :]


def _cache_kv(cache_ckv, cache_kpe_t, w_uk, w_uvt):
    b, n_attn, past, _ = cache_ckv.shape
    hk = N_HEADS * QK_NOPE
    return pl.pallas_call(
        _cache_kv_kernel,
        grid=(n_attn, b),
        in_specs=[
            pl.BlockSpec((1, 1, past, KV_RANK), lambda a, i: (i, a, 0, 0)),
            pl.BlockSpec((1, 1, QK_ROPE, past), lambda a, i: (i, a, 0, 0)),
            pl.BlockSpec((1, KV_RANK, hk), lambda a, i: (a, 0, 0)),
            pl.BlockSpec((1, ATTN_WIDTH, KV_RANK), lambda a, i: (a, 0, 0)),
        ],
        out_specs=[
            pl.BlockSpec((1, 1, N_HEADS, past, 2 * LANES), lambda a, i: (a, i, 0, 0, 0)),
            pl.BlockSpec((1, 1, N_HEADS, V_DIM, past), lambda a, i: (a, i, 0, 0, 0)),
        ],
        out_shape=[
            jax.ShapeDtypeStruct((n_attn, b, N_HEADS, past, 2 * LANES), BF16),
            jax.ShapeDtypeStruct((n_attn, b, N_HEADS, V_DIM, past), BF16),
        ],
        compiler_params=_params(),
        name="cache_kv",
    )(cache_ckv, cache_kpe_t, w_uk, w_uvt)


def _lat_proj_kernel(x_ref, mod_ref, g_ref, cos_ref, sa_ref, sb_ref, w_in_ref, gq_ref, gkv_ref,
                     w_uq_ref, w_uk_ref, w_uvt_ref, q_ref, k_ref, vt_ref, z_ref):
    shift, scale, _ = _mod_parts(mod_ref)
    h = _norm_mod(x_ref[0], g_ref[...], shift, scale).astype(BF16)
    rope = (cos_ref[...], sa_ref[...], sb_ref[...])
    q_heads, k_nope, _, kpe_tile, v_t, _, z = _mla_project(
        h, w_in_ref, gq_ref, gkv_ref, w_uq_ref, w_uk_ref, w_uvt_ref, rope)
    _store_heads(q_ref, (0,), q_heads)
    _store_heads(k_ref, (0,), _key_tiles(k_nope, kpe_tile))
    v_t = v_t.astype(BF16)
    for hd in range(N_HEADS):
        vt_ref[0, hd] = v_t[hd * V_DIM:(hd + 1) * V_DIM, :]
    z_ref[0] = z()


def _lat_proj(x, mod, mod0, layer, a, norm_g, rope, w):
    b, t, _ = x.shape
    tm = TOK_TILE
    hk = N_HEADS * QK_NOPE
    hr = N_HEADS * QK_ROPE
    row = lambda i, j: (i, j, 0)
    head_row = lambda i, j: (i, 0, j, 0)
    tab = pl.BlockSpec((tm, LANES), lambda i, j: (j, 0))
    return pl.pallas_call(
        _lat_proj_kernel,
        grid=(b, t // tm),
        in_specs=[
            pl.BlockSpec((1, tm, D), row),
            pl.BlockSpec((1, 1, 3 * D), lambda i, j: (mod0 + i, 0, 0)),
            _layer((1, D), layer),
            tab, tab, tab,
            _layer((ATTN_IN_PAD, D), a),
            _layer((1, Q_RANK), a),
            _layer((1, KV_RANK), a),
            _layer((Q_RANK, hk + hr), a),
            _layer((KV_RANK, hk), a),
            _layer((ATTN_WIDTH, KV_RANK), a),
        ],
        out_specs=[
            pl.BlockSpec((1, N_HEADS, tm, 2 * LANES), head_row),
            pl.BlockSpec((1, N_HEADS, tm, 2 * LANES), head_row),
            pl.BlockSpec((1, N_HEADS, V_DIM, tm), lambda i, j: (i, 0, 0, j)),
            pl.BlockSpec((1, tm, ATTN_WIDTH), row),
        ],
        out_shape=[
            jax.ShapeDtypeStruct((b, N_HEADS, t, 2 * LANES), BF16),
            jax.ShapeDtypeStruct((b, N_HEADS, t, 2 * LANES), BF16),
            jax.ShapeDtypeStruct((b, N_HEADS, V_DIM, t), BF16),
            jax.ShapeDtypeStruct((b, t, ATTN_WIDTH), F32),
        ],
        compiler_params=_params(),
        name="lat_proj",
    )(x, mod, norm_g, *rope, w["w_in"], w["gq"], w["gkv"], w["w_uq"], w["w_uk"], w["w_uvt"])


def _head_stage(hd_pv, hd_qk, q_ref, kc_ref, kl_ref, vtc_ref, vtl_ref, ot_ref, cur, nxt):
    past = kc_ref.shape[3]
    n_keys = past + kl_ref.shape[2]
    tq = q_ref.shape[2]
    chunks = list(range(0, n_keys, KEY_CHUNK))

    def k_of(hd, c0):
        if c0 < past:
            return kc_ref[0, 0, hd, c0:c0 + KEY_CHUNK]
        return kl_ref[0, hd, c0 - past:c0 - past + KEY_CHUNK]

    def vt_of(hd, c0):
        if c0 < past:
            return vtc_ref[0, 0, hd, :, c0:c0 + KEY_CHUNK]
        return vtl_ref[0, hd, :, c0 - past:c0 - past + KEY_CHUNK]

    col_max = []

    def scores(c0):
        s = _dot_nt(k_of(hd_qk, c0), q_h)
        nxt[0][c0:c0 + KEY_CHUNK] = s
        col_max.append(s.max(axis=0, keepdims=True))

    if hd_qk is not None:
        q_h = q_ref[0, hd_qk]
        for c0 in chunks[:QK_LEAD]:
            scores(c0)
    if hd_pv is not None:
        m = cur[1][...]
        acc = jnp.zeros((V_DIM, tq), F32)
        denom = jnp.zeros((1, tq), F32)
    for i, c0 in enumerate(chunks):
        if hd_pv is not None:
            p = jnp.exp2(cur[0][c0:c0 + KEY_CHUNK] - m)
            denom = denom + p.sum(axis=0, keepdims=True)
            acc = acc + _dot(vt_of(hd_pv, c0), p.astype(BF16))
        if hd_qk is not None and i + QK_LEAD < len(chunks):
            scores(chunks[i + QK_LEAD])
    if hd_qk is not None:
        nxt[1][...] = functools.reduce(jnp.maximum, col_max)
    if hd_pv is not None:
        ot_ref[hd_pv] = acc * (1.0 / denom)


def _lat_attn_kernel(x_ref, mod_ref, q_ref, z_ref, kc_ref, vtc_ref, kl_ref, vtl_ref, w_o_ref,
                     y_ref, s0_ref, s1_ref, m0_ref, m1_ref, ot_ref):
    _, _, gate = _mod_parts(mod_ref)
    stage = functools.partial(_head_stage, q_ref=q_ref, kc_ref=kc_ref, kl_ref=kl_ref,
                              vtc_ref=vtc_ref, vtl_ref=vtl_ref, ot_ref=ot_ref)
    bufs = ((s0_ref, m0_ref), (s1_ref, m1_ref))

    for hd in range(-1, N_HEADS):
        stage(hd if hd >= 0 else None, hd + 1 if hd + 1 < N_HEADS else None,
              cur=bufs[hd % 2], nxt=bufs[(hd + 1) % 2])

    o = jnp.concatenate([ot_ref[hd].T for hd in range(N_HEADS)], axis=1)
    y_ref[0] = _gated_residual(x_ref[0], o, z_ref[0], gate, w_o_ref)


def _lat_attn(x, mod, mod0, a, q, z, kc, vtc, kl, vtl, w_o):
    b, t, _ = x.shape
    tq = Q_TILE
    past = kc.shape[3]
    row = lambda i, j: (i, j, 0)
    per_b = lambda i, j: (i, 0, 0, 0)
    cache = lambda i, j: (a, i, 0, 0, 0)
    return pl.pallas_call(
        _lat_attn_kernel,
        grid=(b, t // tq),
        in_specs=[
            pl.BlockSpec((1, tq, D), row),
            pl.BlockSpec((1, 1, 3 * D), lambda i, j: (mod0 + i, 0, 0)),
            pl.BlockSpec((1, N_HEADS, tq, 2 * LANES), lambda i, j: (i, 0, j, 0)),
            pl.BlockSpec((1, tq, ATTN_WIDTH), row),
            pl.BlockSpec((1, 1, N_HEADS, past, 2 * LANES), cache),
            pl.BlockSpec((1, 1, N_HEADS, V_DIM, past), cache),
            pl.BlockSpec((1, N_HEADS, t, 2 * LANES), per_b),
            pl.BlockSpec((1, N_HEADS, V_DIM, t), per_b),
            _layer((ATTN_WIDTH, D), a),
        ],
        out_specs=pl.BlockSpec((1, tq, D), row),
        out_shape=jax.ShapeDtypeStruct((b, t, D), F32),
        scratch_shapes=[
            pltpu.VMEM((past + t, tq), F32), pltpu.VMEM((past + t, tq), F32),
            pltpu.VMEM((1, tq), F32), pltpu.VMEM((1, tq), F32),
            pltpu.VMEM((N_HEADS, V_DIM, tq), F32),
        ],
        compiler_params=_params(),
        name="lat_attn",
    )(x, mod, q, z, kc, vtc, kl, vtl, w_o)


def _mlp_kernel(x_ref, mod_ref, g_ref, w_in_ref, vg_ref, vb_ref, w_s_ref, b_s_ref, w_o_ref,
                fg_ref, y_ref, v_ref, uz_ref, *, final_norm):
    x = x_ref[0]
    tm = x.shape[0]
    shift, scale, gate = _mod_parts(mod_ref)
    h = _norm_mod(x, g_ref[...], shift, scale).astype(BF16)

    s1 = [jnp.zeros((tm, LANES), F32)]
    s2 = [jnp.zeros((tm, LANES), F32)]
    shift_c = []

    def v_group(g):
        vcols = slice(MLP_WIDTH + g * GROUP_W, MLP_WIDTH + (g + 1) * GROUP_W)
        v = jax.nn.gelu(_dot(h, w_in_ref[:, vcols]))
        v_ref[:, g * GROUP_W:(g + 1) * GROUP_W] = v
        tiles = [v[:, j * LANES:(j + 1) * LANES] for j in range(GROUP_W // LANES)]
        if g == 0:
            shift_c.append(jnp.sum(sum(tiles[1:], tiles[0]), axis=-1, keepdims=True) * (1.0 / GROUP_W))
        for t in tiles:
            d = t - shift_c[0]
            s1[0] = s1[0] + d
            s2[0] = s2[0] + d * d

    def uz_group(g):
        cols = slice(g * GROUP_W, (g + 1) * GROUP_W)
        zcols = slice(2 * MLP_WIDTH + g * GROUP_W, 2 * MLP_WIDTH + (g + 1) * GROUP_W)
        u = jax.nn.gelu(_dot(h, w_in_ref[:, cols]))
        z = _dot(h, w_in_ref[:, zcols])
        uz_ref[:, cols] = u * _silu(z)

    for g in range(V_LEAD):
        v_group(g)
    for g in range(MLP_GROUPS):
        if g + V_LEAD < MLP_GROUPS:
            v_group(g + V_LEAD)
        if g == MLP_GROUPS - V_LEAD:
            m1 = jnp.sum(s1[0], axis=-1, keepdims=True) * (1.0 / MLP_WIDTH)
            mu = shift_c[0] + m1
            var = jnp.sum(s2[0], axis=-1, keepdims=True) * (1.0 / MLP_WIDTH) - m1 * m1
            rstd = lax.rsqrt(var + EPS)
        uz_group(g)

    def spatial(g):
        cols = slice(g * GROUP_W, (g + 1) * GROUP_W)
        vn = ((v_ref[:, cols] - mu) * rstd * vg_ref[:, cols] + vb_ref[:, cols]).astype(BF16)
        bias = b_s_ref[:, g:g + 1]
        return jnp.concatenate(
            [_dot(w_s_ref[g], vn[c * CHUNK:(c + 1) * CHUNK]) + bias for c in range(tm // CHUNK)],
            axis=0)

    gates = [spatial(g) for g in range(SPATIAL_LEAD)]
    acc = None
    for g in range(MLP_GROUPS):
        cols = slice(g * GROUP_W, (g + 1) * GROUP_W)
        part = _dot((uz_ref[:, cols] * gates[g]).astype(BF16), w_o_ref[cols, :])
        acc = part if acc is None else acc + part
        if g + SPATIAL_LEAD < MLP_GROUPS:
            gates.append(spatial(g + SPATIAL_LEAD))

    y = x + gate * acc
    if final_norm:
        y = _rms(y) * fg_ref[...]
    y_ref[0] = y


def _mlp_layer(x, mod, mod0, layer, m, norm_g, w, final_g, final_norm):
    b, t, _ = x.shape
    tm = TOK_TILE
    row = lambda i, j: (i, j, 0)
    return pl.pallas_call(
        functools.partial(_mlp_kernel, final_norm=final_norm),
        grid=(b, t // tm),
        in_specs=[
            pl.BlockSpec((1, tm, D), row),
            pl.BlockSpec((1, 1, 3 * D), lambda i, j: (mod0 + i, 0, 0)),
            _layer((1, D), layer),
            _layer((D, 3 * MLP_WIDTH), m, single_buffer=True),
            _layer((1, MLP_WIDTH), m),
            _layer((1, MLP_WIDTH), m),
            _layer((MLP_GROUPS, CHUNK, CHUNK), m),
            _layer((CHUNK, MLP_GROUPS), m),
            _layer((MLP_WIDTH, D), m, single_buffer=True),
            _full((1, D)),
        ],
        out_specs=pl.BlockSpec((1, tm, D), row),
        out_shape=jax.ShapeDtypeStruct((b, t, D), F32),
        scratch_shapes=[pltpu.VMEM((tm, MLP_WIDTH), F32), pltpu.VMEM((tm, MLP_WIDTH), F32)],
        compiler_params=_params(),
        name="mlp",
    )(x, mod, norm_g, w["w_in"], w["vg"], w["vb"], w["w_s"], w["b_s_t"], w["w_o"], final_g)


def _rope_tables(n_tokens):
    pos = np.arange(n_tokens)
    inv = 1.0 / (ROPE_THETA ** (np.arange(0, AXIS_ROPE, 2, dtype=np.float64) / AXIS_ROPE))
    ang_r, ang_c = (pos // GRID_W)[:, None] * inv, (pos % GRID_W)[:, None] * inv
    cr, sr, cc, sc = np.cos(ang_r), np.sin(ang_r), np.cos(ang_c), np.sin(ang_c)
    zero = np.zeros_like(sr)
    cos = np.concatenate([cr, cr, cc, cc], axis=1)
    sin_a = np.concatenate([-sr, zero, -sc, zero], axis=1)
    sin_b = np.concatenate([zero, sr, zero, sc], axis=1)
    return tuple(jnp.asarray(np.tile(t, (1, LANES // QK_ROPE)), F32) for t in (cos, sin_a, sin_b))


def _attn_weights(w_in, gq, gkv, w_uq, w_ukv, w_o):
    n = w_in.shape[0]
    kpe0 = Q_RANK + KV_RANK
    w_in_t = jnp.swapaxes(w_in, 1, 2)
    w_in_p = jnp.concatenate(
        [w_in_t[:, :kpe0 + QK_ROPE], w_in_t[:, kpe0:kpe0 + QK_ROPE], w_in_t[:, kpe0 + QK_ROPE:]], axis=1)
    ukv = w_ukv.reshape(n, KV_RANK, N_HEADS, QK_NOPE + V_DIM)
    w_uk = ukv[..., :QK_NOPE].reshape(n, KV_RANK, -1)
    w_uvt = jnp.swapaxes(ukv[..., QK_NOPE:].reshape(n, KV_RANK, -1), 1, 2)
    return dict(w_in=w_in_p.astype(BF16), gq=gq[:, None, :], gkv=gkv[:, None, :],
                w_uq=w_uq.astype(BF16), w_uk=w_uk.astype(BF16), w_uvt=w_uvt.astype(BF16),
                w_o=w_o.astype(BF16))


def _mlp_weights(w_in_bf16, vg, vb, w_s, b_s, w_o_bf16):
    return dict(w_in=w_in_bf16, vg=vg[:, None, :], vb=vb[:, None, :],
                w_s=w_s.astype(BF16), b_s_t=jnp.swapaxes(b_s, 1, 2), w_o=w_o_bf16)


def kernel(x_prompt, x_sample, cache_ckv, cache_kpe, c, c_ctx, norm_g, w_mod, b_mod, attn_w_in, attn_q_norm_g, attn_kv_norm_g, attn_w_uq, attn_w_ukv, attn_w_o, mlp_w_in, mlp_v_norm_g, mlp_v_norm_b, mlp_w_s, mlp_b_s, mlp_w_o, final_norm_g):
    depth = norm_g.shape[0]
    bc, tc, _ = x_prompt.shape
    bl, tl, _ = x_sample.shape
    if depth % 2:
        raise NotImplementedError("final RMSNorm is fused into a trailing gMLP layer")

    cond8 = jnp.concatenate([c_ctx[None, :], c, jnp.zeros((8 - 1 - bl, D), F32)], axis=0)
    mods = _modulation(cond8, w_mod, b_mod).reshape(depth * 8, 1, 3 * D)
    rope = _rope_tables(tl)
    norm_g = norm_g[:, None, :]
    final_g = final_norm_g.reshape(1, D)
    aw = _attn_weights(attn_w_in, attn_q_norm_g, attn_kv_norm_g, attn_w_uq, attn_w_ukv, attn_w_o)
    kc, vtc = _cache_kv(cache_ckv, jnp.swapaxes(cache_kpe, 2, 3), aw["w_uk"], aw["w_uvt"])

    xc, xl = x_prompt, x_sample
    cache_out = ()
    for layer in range(depth):
        mod0 = 8 * layer
        if layer % 2 == 0:
            a = layer // 2
            if layer == 0:
                xc, *cache_out, w_in_b, w_o_b = _ctx_attn_layer(
                    xc, mods, mod0, layer, a, norm_g, aw, cache_out, riders=(mlp_w_in, mlp_w_o))
                mw = _mlp_weights(w_in_b, mlp_v_norm_g, mlp_v_norm_b, mlp_w_s, mlp_b_s, w_o_b)
            else:
                xc, *cache_out = _ctx_attn_layer(xc, mods, mod0, layer, a, norm_g, aw, cache_out)
            q, kl, vtl, z = _lat_proj(xl, mods, mod0 + 1, layer, a, norm_g, rope, aw)
            xl = _lat_attn(xl, mods, mod0 + 1, a, q, z, kc, vtc, kl, vtl, aw["w_o"])
        else:
            m = layer // 2
            last = layer == depth - 1
            xc = _mlp_layer(xc.reshape(1, bc * tc, D), mods, mod0, layer, m, norm_g, mw, final_g,
                            last).reshape(bc, tc, D)
            xl = _mlp_layer(xl, mods, mod0 + 1, layer, m, norm_g, mw, final_g, last)
    new_ckv, new_kpe_t = cache_out
    return xc, xl, new_ckv, jnp.swapaxes(new_kpe_t, 2, 3)
```

```python
import functools
import math

import jax
import jax.numpy as jnp
import numpy as np
from jax import lax
from jax.experimental import pallas as pl
from jax.experimental.pallas import tpu as pltpu

F32 = jnp.float32
BF16 = jnp.bfloat16

D = 1024
N_HEADS = 8
Q_RANK = 512
KV_RANK = 256
QK_NOPE = 128
QK_ROPE = 64
V_DIM = 128
ATTN_WIDTH = N_HEADS * V_DIM
AXIS_ROPE = QK_ROPE // 2
ROPE_THETA = 10000.0
GRID_W = 64
CHUNK = 128
MLP_GROUPS = 8
MLP_WIDTH = 2 * D
GROUP_W = MLP_WIDTH // MLP_GROUPS
EPS = 1e-6
QK_SCALE = float((QK_NOPE + QK_ROPE) ** -0.5 * math.log2(math.e))

LANES = 128
ATTN_IN_PAD = Q_RANK + KV_RANK + LANES + ATTN_WIDTH
VMEM_LIMIT = 56 * 1024 * 1024

TOK_TILE = 512
Q_TILE = 256
CTX_ROWS = 2
QK_LEAD = 3
V_LEAD = 1
SPATIAL_LEAD = 2
KEY_CHUNK = 512


def _params():
    return pltpu.CompilerParams(vmem_limit_bytes=VMEM_LIMIT)


def _dot(a, b):
    return jnp.dot(a, b, preferred_element_type=F32)


def _dot_nt(a, b):
    return lax.dot_general(a, b, (((1,), (1,)), ((), ())), preferred_element_type=F32)


def _rms(x):
    return x * lax.rsqrt(jnp.mean(x * x, axis=-1, keepdims=True) + EPS)


def _silu(x):
    return x * jax.nn.sigmoid(x)


def _mod_parts(mod_ref):
    m = mod_ref[0]
    return m[:, :D], m[:, D:2 * D], m[:, 2 * D:]


def _norm_mod(x, g, shift, scale):
    return _rms(x) * (g * (1.0 + scale)) + shift


def _full(shape):
    n = len(shape)
    return pl.BlockSpec(shape, lambda *_: (0,) * n)


def _layer(shape, l, single_buffer=False):
    n = len(shape)
    mode = dict(pipeline_mode=pl.Buffered(1)) if single_buffer else {}
    return pl.BlockSpec((None,) + tuple(shape), lambda *_: (l,) + (0,) * n, **mode)


def _mod_kernel(cond_ref, w_ref, b_ref, o_ref):
    a = _silu(cond_ref[...]).astype(BF16)
    o_ref[0] = _dot(a, w_ref[0].astype(BF16)) + b_ref[0]


def _modulation(cond8, w_mod, b_mod):
    depth = w_mod.shape[0]
    tn = 1024
    return pl.pallas_call(
        _mod_kernel,
        grid=(depth, 3 * D // tn),
        in_specs=[
            pl.BlockSpec((8, D), lambda l, j: (0, 0)),
            pl.BlockSpec((1, D, tn), lambda l, j: (l, 0, j)),
            pl.BlockSpec((1, 1, tn), lambda l, j: (l, 0, j)),
        ],
        out_specs=pl.BlockSpec((1, 8, tn), lambda l, j: (l, 0, j)),
        out_shape=jax.ShapeDtypeStruct((depth, 8, 3 * D), F32),
        compiler_params=_params(),
        name="modulation",
    )(cond8, w_mod, b_mod.reshape(depth, 1, 3 * D))


def _rope(x, cos, sin_a, sin_b):
    return x * cos + pltpu.roll(x, LANES - 16, 1) * sin_a + pltpu.roll(x, 16, 1) * sin_b


def _kpe_tile(kpe2):
    lane = lax.broadcasted_iota(jnp.int32, kpe2.shape, 1)
    return jnp.where(lane < QK_ROPE, kpe2, jnp.zeros_like(kpe2))


def _q_heads(q, rope):
    lane = lax.broadcasted_iota(jnp.int32, (q.shape[0], LANES), 1)
    heads = []
    for pair in range(N_HEADS // 2):
        t0, t1, t2 = (q[:, (3 * pair + j) * LANES:(3 * pair + j + 1) * LANES] for j in range(3))
        nope_odd = jnp.where(lane < QK_ROPE, pltpu.roll(t1, QK_ROPE, 1), pltpu.roll(t2, QK_ROPE, 1))
        rot = jnp.where(lane < QK_ROPE, t1, t2)
        if rope is not None:
            rot = _rope(rot, *rope)
        heads += [(t0, rot), (nope_odd, pltpu.roll(rot, QK_ROPE, 1))]
    return heads


def _mla_project(h, w_in_ref, gq_ref, gkv_ref, w_uq_ref, w_uk_ref, w_uvt_ref, rope, defer_z=False):
    z_row0 = Q_RANK + KV_RANK + LANES
    proj = _dot_nt(h, w_in_ref[:z_row0 if defer_z else ATTN_IN_PAD, :])
    c_q = proj[:, :Q_RANK]
    c_kv = proj[:, Q_RANK:Q_RANK + KV_RANK]
    kpe2 = proj[:, Q_RANK + KV_RANK:z_row0]
    z = (lambda: _dot_nt(h, w_in_ref[z_row0:, :])) if defer_z else (lambda: proj[:, z_row0:])
    q = _dot((_rms(c_q) * gq_ref[...]).astype(BF16), w_uq_ref[...]) * QK_SCALE
    ckv = _rms(c_kv) * gkv_ref[...]
    k_nope = _dot(ckv.astype(BF16), w_uk_ref[...])
    v_t = lambda: _dot(w_uvt_ref[...], ckv.T.astype(BF16))
    kpe_rot = kpe2 if rope is None else _rope(kpe2, *rope)
    return _q_heads(q, rope), k_nope, kpe2, _kpe_tile(kpe_rot), v_t, ckv, z


def _gated_residual(x, o, z, gate, w_o_ref):
    mix = _dot((o * _silu(z)).astype(BF16), w_o_ref[...])
    return x + gate * mix


def _ctx_attn_kernel(*refs, n_prev, n_cast):
    (x_ref, mod_ref, g_ref, w_in_ref, gq_ref, gkv_ref, w_uq_ref, w_uk_ref, w_uvt_ref,
     w_o_ref) = refs[:10]
    refs = refs[10:]
    prev, refs = refs[:2 * bool(n_prev)], refs[2 * bool(n_prev):]
    cast_in, refs = refs[:n_cast], refs[n_cast:]
    (y_ref, ckv_ref, kpet_ref), refs = refs[:3], refs[3:]
    cast_out, (s_ref, p_ref) = refs[:n_cast], refs[n_cast:]
    for src, dst in zip(cast_in, cast_out):
        dst[...] = src[...].astype(BF16)
    shift, scale, gate = _mod_parts(mod_ref)
    if n_prev:
        ckv_ref[:, :n_prev] = prev[0][...]
        kpet_ref[:, :n_prev] = prev[1][...]
    for r in range(x_ref.shape[0]):
        x = x_ref[r]
        h = _norm_mod(x, g_ref[...], shift, scale).astype(BF16)
        q_heads, k_nope, kpe2, kpe_tile, v_t, ckv, z = _mla_project(
            h, w_in_ref, gq_ref, gkv_ref, w_uq_ref, w_uk_ref, w_uvt_ref, None, defer_z=True)
        ckv_ref[r, n_prev] = ckv
        kpet_ref[r, n_prev] = kpe2.T[:QK_ROPE]
        kpe_tile = kpe_tile.astype(BF16)
        k_nope = k_nope.astype(BF16)
        for hd in range(N_HEADS):
            q_h = jnp.concatenate([t.astype(BF16) for t in q_heads[hd]], axis=1)
            k_h = jnp.concatenate([k_nope[:, hd * LANES:(hd + 1) * LANES], kpe_tile], axis=1)
            s_ref[r, hd] = _dot_nt(k_h, q_h)
        v_t = v_t().astype(BF16)
        z = z()
        inv = []
        for hd in range(N_HEADS):
            s = s_ref[r, hd]
            p = jnp.exp2(s - s.max(axis=0, keepdims=True))
            inv.append(1.0 / p.sum(axis=0, keepdims=True))
            p_ref[r, hd] = p.astype(BF16)
        o_t = [_dot(v_t[hd * LANES:(hd + 1) * LANES, :], p_ref[r, hd]) * inv[hd]
               for hd in range(N_HEADS)]
        o = jnp.concatenate([t.T for t in o_t], axis=1)
        y_ref[r] = _gated_residual(x, o, z, gate, w_o_ref)


def _ctx_attn_layer(x, mod, mod0, layer, a, norm_g, w, prev, riders=()):
    b, t, _ = x.shape
    rows = CTX_ROWS
    steps = b // rows
    blk = lambda i: (i, 0, 0, 0)
    chunk = lambda p: pl.BlockSpec((p.shape[0], p.shape[1] // steps, p.shape[2]), lambda i: (0, i, 0))
    return pl.pallas_call(
        functools.partial(_ctx_attn_kernel, n_prev=a, n_cast=len(riders)),
        grid=(steps,),
        in_specs=[
            pl.BlockSpec((rows, t, D), lambda i: (i, 0, 0)),
            pl.BlockSpec((1, 1, 3 * D), lambda i: (mod0, 0, 0)),
            _layer((1, D), layer),
            _layer((ATTN_IN_PAD, D), a),
            _layer((1, Q_RANK), a),
            _layer((1, KV_RANK), a),
            _layer((Q_RANK, N_HEADS * (QK_NOPE + QK_ROPE)), a),
            _layer((KV_RANK, N_HEADS * QK_NOPE), a),
            _layer((N_HEADS * V_DIM, KV_RANK), a),
            _layer((ATTN_WIDTH, D), a),
        ] + ([pl.BlockSpec((rows, a, t, KV_RANK), blk), pl.BlockSpec((rows, a, QK_ROPE, t), blk)]
             if a else []) + [chunk(p) for p in riders],
        out_specs=[
            pl.BlockSpec((rows, t, D), lambda i: (i, 0, 0)),
            pl.BlockSpec((rows, a + 1, t, KV_RANK), blk),
            pl.BlockSpec((rows, a + 1, QK_ROPE, t), blk),
        ] + [chunk(p) for p in riders],
        out_shape=[
            jax.ShapeDtypeStruct((b, t, D), F32),
            jax.ShapeDtypeStruct((b, a + 1, t, KV_RANK), F32),
            jax.ShapeDtypeStruct((b, a + 1, QK_ROPE, t), F32),
        ] + [jax.ShapeDtypeStruct(p.shape, BF16) for p in riders],
        scratch_shapes=[pltpu.VMEM((rows, N_HEADS, t, t), F32), pltpu.VMEM((rows, N_HEADS, t, t), BF16)],
        compiler_params=_params(),
        name="ctx_attn",
    )(x, mod, norm_g, w["w_in"], w["gq"], w["gkv"], w["w_uq"], w["w_uk"], w["w_uvt"], w["w_o"], *prev,
      *riders)


def _store_heads(ref, lead, tiles):
    for hd, (nope, rot) in enumerate(tiles):
        ref[lead + (hd, slice(None), slice(0, LANES))] = nope.astype(BF16)
        ref[lead + (hd, slice(None), slice(LANES, 2 * LANES))] = rot.astype(BF16)


def _key_tiles(k_nope, kpe_tile):
    return [(k_nope[:, hd * LANES:(hd + 1) * LANES], kpe_tile) for hd in range(N_HEADS)]


def _cache_kv_kernel(ckv_ref, kpet_ref, w_uk_ref, w_uvt_ref, k_ref, vt_ref):
    ckv = ckv_ref[0, 0]
    k_nope = _dot(ckv.astype(BF16), w_uk_ref[0])
    kpe_t = kpet_ref[0, 0]
    kpe2 = jnp.concatenate([kpe_t] * (LANES // QK_ROPE), axis=0).T
    _store_heads(k_ref, (0, 0), _key_tiles(k_nope, _kpe_tile(kpe2)))
    v_t = _dot(w_uvt_ref[0], ckv.T.astype(BF16)).astype(BF16)
    for hd in range(N_HEADS):
        vt_ref[0, 0, hd] = v_t[hd * V_DIM:(hd + 1) * V_DIM, :]


def _cache_kv(cache_ckv, cache_kpe_t, w_uk, w_uvt):
    b, n_attn, past, _ = cache_ckv.shape
    hk = N_HEADS * QK_NOPE
    return pl.pallas_call(
        _cache_kv_kernel,
        grid=(n_attn, b),
        in_specs=[
            pl.BlockSpec((1, 1, past, KV_RANK), lambda a, i: (i, a, 0, 0)),
            pl.BlockSpec((1, 1, QK_ROPE, past), lambda a, i: (i, a, 0, 0)),
            pl.BlockSpec((1, KV_RANK, hk), lambda a, i: (a, 0, 0)),
            pl.BlockSpec((1, ATTN_WIDTH, KV_RANK), lambda a, i: (a, 0, 0)),
        ],
        out_specs=[
            pl.BlockSpec((1, 1, N_HEADS, past, 2 * LANES), lambda a, i: (a, i, 0, 0, 0)),
            pl.BlockSpec((1, 1, N_HEADS, V_DIM, past), lambda a, i: (a, i, 0, 0, 0)),
        ],
        out_shape=[
            jax.ShapeDtypeStruct((n_attn, b, N_HEADS, past, 2 * LANES), BF16),
            jax.ShapeDtypeStruct((n_attn, b, N_HEADS, V_DIM, past), BF16),
        ],
        compiler_params=_params(),
        name="cache_kv",
    )(cache_ckv, cache_kpe_t, w_uk, w_uvt)


def _lat_proj_kernel(x_ref, mod_ref, g_ref, cos_ref, sa_ref, sb_ref, w_in_ref, gq_ref, gkv_ref,
                     w_uq_ref, w_uk_ref, w_uvt_ref, q_ref, k_ref, vt_ref, z_ref):
    shift, scale, _ = _mod_parts(mod_ref)
    h = _norm_mod(x_ref[0], g_ref[...], shift, scale).astype(BF16)
    rope = (cos_ref[...], sa_ref[...], sb_ref[...])
    q_heads, k_nope, _, kpe_tile, v_t, _, z = _mla_project(
        h, w_in_ref, gq_ref, gkv_ref, w_uq_ref, w_uk_ref, w_uvt_ref, rope)
    _store_heads(q_ref, (0,), q_heads)
    _store_heads(k_ref, (0,), _key_tiles(k_nope, kpe_tile))
    v_t = v_t().astype(BF16)
    for hd in range(N_HEADS):
        vt_ref[0, hd] = v_t[hd * V_DIM:(hd + 1) * V_DIM, :]
    z_ref[0] = z()


def _lat_proj(x, mod, mod0, layer, a, norm_g, rope, w):
    b, t, _ = x.shape
    tm = TOK_TILE
    hk = N_HEADS * QK_NOPE
    hr = N_HEADS * QK_ROPE
    row = lambda i, j: (i, j, 0)
    head_row = lambda i, j: (i, 0, j, 0)
    tab = pl.BlockSpec((tm, LANES), lambda i, j: (j, 0))
    return pl.pallas_call(
        _lat_proj_kernel,
        grid=(b, t // tm),
        in_specs=[
            pl.BlockSpec((1, tm, D), row),
            pl.BlockSpec((1, 1, 3 * D), lambda i, j: (mod0 + i, 0, 0)),
            _layer((1, D), layer),
            tab, tab, tab,
            _layer((ATTN_IN_PAD, D), a),
            _layer((1, Q_RANK), a),
            _layer((1, KV_RANK), a),
            _layer((Q_RANK, hk + hr), a),
            _layer((KV_RANK, hk), a),
            _layer((ATTN_WIDTH, KV_RANK), a),
        ],
        out_specs=[
            pl.BlockSpec((1, N_HEADS, tm, 2 * LANES), head_row),
            pl.BlockSpec((1, N_HEADS, tm, 2 * LANES), head_row),
            pl.BlockSpec((1, N_HEADS, V_DIM, tm), lambda i, j: (i, 0, 0, j)),
            pl.BlockSpec((1, tm, ATTN_WIDTH), row),
        ],
        out_shape=[
            jax.ShapeDtypeStruct((b, N_HEADS, t, 2 * LANES), BF16),
            jax.ShapeDtypeStruct((b, N_HEADS, t, 2 * LANES), BF16),
            jax.ShapeDtypeStruct((b, N_HEADS, V_DIM, t), BF16),
            jax.ShapeDtypeStruct((b, t, ATTN_WIDTH), F32),
        ],
        compiler_params=_params(),
        name="lat_proj",
    )(x, mod, norm_g, *rope, w["w_in"], w["gq"], w["gkv"], w["w_uq"], w["w_uk"], w["w_uvt"])


def _head_stage(hd_pv, hd_qk, q_ref, kc_ref, kl_ref, vtc_ref, vtl_ref, ot_ref, cur, nxt):
    past = kc_ref.shape[3]
    n_keys = past + kl_ref.shape[2]
    tq = q_ref.shape[2]
    chunks = list(range(0, n_keys, KEY_CHUNK))

    def k_of(hd, c0):
        if c0 < past:
            return kc_ref[0, 0, hd, c0:c0 + KEY_CHUNK]
        return kl_ref[0, hd, c0 - past:c0 - past + KEY_CHUNK]

    def vt_of(hd, c0):
        if c0 < past:
            return vtc_ref[0, 0, hd, :, c0:c0 + KEY_CHUNK]
        return vtl_ref[0, hd, :, c0 - past:c0 - past + KEY_CHUNK]

    col_max = []

    def scores(c0):
        s = _dot_nt(k_of(hd_qk, c0), q_h)
        nxt[0][c0:c0 + KEY_CHUNK] = s
        col_max.append(s.max(axis=0, keepdims=True))

    if hd_qk is not None:
        q_h = q_ref[0, hd_qk]
        for c0 in chunks[:QK_LEAD]:
            scores(c0)
    if hd_pv is not None:
        m = cur[1][...]
        acc = jnp.zeros((V_DIM, tq), F32)
        denom = jnp.zeros((1, tq), F32)
    for i, c0 in enumerate(chunks):
        if hd_pv is not None:
            p = jnp.exp2(cur[0][c0:c0 + KEY_CHUNK] - m)
            denom = denom + p.sum(axis=0, keepdims=True)
            acc = acc + _dot(vt_of(hd_pv, c0), p.astype(BF16))
        if hd_qk is not None and i + QK_LEAD < len(chunks):
            scores(chunks[i + QK_LEAD])
    if hd_qk is not None:
        nxt[1][...] = functools.reduce(jnp.maximum, col_max)
    if hd_pv is not None:
        ot_ref[hd_pv] = acc * (1.0 / denom)


def _lat_attn_kernel(x_ref, mod_ref, q_ref, z_ref, kc_ref, vtc_ref, kl_ref, vtl_ref, w_o_ref,
                     y_ref, s0_ref, s1_ref, m0_ref, m1_ref, ot_ref):
    _, _, gate = _mod_parts(mod_ref)
    stage = functools.partial(_head_stage, q_ref=q_ref, kc_ref=kc_ref, kl_ref=kl_ref,
                              vtc_ref=vtc_ref, vtl_ref=vtl_ref, ot_ref=ot_ref)
    bufs = ((s0_ref, m0_ref), (s1_ref, m1_ref))

    for hd in range(-1, N_HEADS):
        stage(hd if hd >= 0 else None, hd + 1 if hd + 1 < N_HEADS else None,
              cur=bufs[hd % 2], nxt=bufs[(hd + 1) % 2])

    o = jnp.concatenate([ot_ref[hd].T for hd in range(N_HEADS)], axis=1)
    y_ref[0] = _gated_residual(x_ref[0], o, z_ref[0], gate, w_o_ref)


def _lat_attn(x, mod, mod0, a, q, z, kc, vtc, kl, vtl, w_o):
    b, t, _ = x.shape
    tq = Q_TILE
    past = kc.shape[3]
    row = lambda i, j: (i, j, 0)
    per_b = lambda i, j: (i, 0, 0, 0)
    cache = lambda i, j: (a, i, 0, 0, 0)
    return pl.pallas_call(
        _lat_attn_kernel,
        grid=(b, t // tq),
        in_specs=[
            pl.BlockSpec((1, tq, D), row),
            pl.BlockSpec((1, 1, 3 * D), lambda i, j: (mod0 + i, 0, 0)),
            pl.BlockSpec((1, N_HEADS, tq, 2 * LANES), lambda i, j: (i, 0, j, 0)),
            pl.BlockSpec((1, tq, ATTN_WIDTH), row),
            pl.BlockSpec((1, 1, N_HEADS, past, 2 * LANES), cache),
            pl.BlockSpec((1, 1, N_HEADS, V_DIM, past), cache),
            pl.BlockSpec((1, N_HEADS, t, 2 * LANES), per_b),
            pl.BlockSpec((1, N_HEADS, V_DIM, t), per_b),
            _layer((ATTN_WIDTH, D), a),
        ],
        out_specs=pl.BlockSpec((1, tq, D), row),
        out_shape=jax.ShapeDtypeStruct((b, t, D), F32),
        scratch_shapes=[
            pltpu.VMEM((past + t, tq), F32), pltpu.VMEM((past + t, tq), F32),
            pltpu.VMEM((1, tq), F32), pltpu.VMEM((1, tq), F32),
            pltpu.VMEM((N_HEADS, V_DIM, tq), F32),
        ],
        compiler_params=_params(),
        name="lat_attn",
    )(x, mod, q, z, kc, vtc, kl, vtl, w_o)


def _mlp_kernel(x_ref, mod_ref, g_ref, w_in_ref, vg_ref, vb_ref, w_s_ref, b_s_ref, w_o_ref,
                fg_ref, y_ref, v_ref, uz_ref, *, final_norm):
    x = x_ref[0]
    tm = x.shape[0]
    shift, scale, gate = _mod_parts(mod_ref)
    h = _norm_mod(x, g_ref[...], shift, scale).astype(BF16)

    s1 = [jnp.zeros((tm, LANES), F32)]
    s2 = [jnp.zeros((tm, LANES), F32)]
    shift_c = []

    def v_group(g):
        vcols = slice(MLP_WIDTH + g * GROUP_W, MLP_WIDTH + (g + 1) * GROUP_W)
        v = jax.nn.gelu(_dot(h, w_in_ref[:, vcols]))
        v_ref[:, g * GROUP_W:(g + 1) * GROUP_W] = v
        tiles = [v[:, j * LANES:(j + 1) * LANES] for j in range(GROUP_W // LANES)]
        if g == 0:
            shift_c.append(jnp.sum(sum(tiles[1:], tiles[0]), axis=-1, keepdims=True) * (1.0 / GROUP_W))
        for t in tiles:
            d = t - shift_c[0]
            s1[0] = s1[0] + d
            s2[0] = s2[0] + d * d

    def uz_group(g):
        cols = slice(g * GROUP_W, (g + 1) * GROUP_W)
        zcols = slice(2 * MLP_WIDTH + g * GROUP_W, 2 * MLP_WIDTH + (g + 1) * GROUP_W)
        u = jax.nn.gelu(_dot(h, w_in_ref[:, cols]))
        z = _dot(h, w_in_ref[:, zcols])
        uz_ref[:, cols] = u * _silu(z)

    for g in range(V_LEAD):
        v_group(g)
    for g in range(MLP_GROUPS):
        if g + V_LEAD < MLP_GROUPS:
            v_group(g + V_LEAD)
        if g == MLP_GROUPS - V_LEAD:
            m1 = jnp.sum(s1[0], axis=-1, keepdims=True) * (1.0 / MLP_WIDTH)
            mu = shift_c[0] + m1
            var = jnp.sum(s2[0], axis=-1, keepdims=True) * (1.0 / MLP_WIDTH) - m1 * m1
            rstd = lax.rsqrt(var + EPS)
        uz_group(g)

    def spatial(g):
        cols = slice(g * GROUP_W, (g + 1) * GROUP_W)
        vn = ((v_ref[:, cols] - mu) * rstd * vg_ref[:, cols] + vb_ref[:, cols]).astype(BF16)
        bias = b_s_ref[:, g:g + 1]
        return jnp.concatenate(
            [_dot(w_s_ref[g], vn[c * CHUNK:(c + 1) * CHUNK]) + bias for c in range(tm // CHUNK)],
            axis=0)

    gates = [spatial(g) for g in range(SPATIAL_LEAD)]
    acc = None
    for g in range(MLP_GROUPS):
        cols = slice(g * GROUP_W, (g + 1) * GROUP_W)
        part = _dot((uz_ref[:, cols] * gates[g]).astype(BF16), w_o_ref[cols, :])
        acc = part if acc is None else acc + part
        if g + SPATIAL_LEAD < MLP_GROUPS:
            gates.append(spatial(g + SPATIAL_LEAD))

    y = x + gate * acc
    if final_norm:
        y = _rms(y) * fg_ref[...]
    y_ref[0] = y


def _mlp_layer(x, mod, mod0, layer, m, norm_g, w, final_g, final_norm):
    b, t, _ = x.shape
    tm = TOK_TILE
    row = lambda i, j: (i, j, 0)
    return pl.pallas_call(
        functools.partial(_mlp_kernel, final_norm=final_norm),
        grid=(b, t // tm),
        in_specs=[
            pl.BlockSpec((1, tm, D), row),
            pl.BlockSpec((1, 1, 3 * D), lambda i, j: (mod0 + i, 0, 0)),
            _layer((1, D), layer),
            _layer((D, 3 * MLP_WIDTH), m, single_buffer=True),
            _layer((1, MLP_WIDTH), m),
            _layer((1, MLP_WIDTH), m),
            _layer((MLP_GROUPS, CHUNK, CHUNK), m),
            _layer((CHUNK, MLP_GROUPS), m),
            _layer((MLP_WIDTH, D), m, single_buffer=True),
            _full((1, D)),
        ],
        out_specs=pl.BlockSpec((1, tm, D), row),
        out_shape=jax.ShapeDtypeStruct((b, t, D), F32),
        scratch_shapes=[pltpu.VMEM((tm, MLP_WIDTH), F32), pltpu.VMEM((tm, MLP_WIDTH), F32)],
        compiler_params=_params(),
        name="mlp",
    )(x, mod, norm_g, w["w_in"], w["vg"], w["vb"], w["w_s"], w["b_s_t"], w["w_o"], final_g)


def _rope_tables(n_tokens):
    pos = np.arange(n_tokens)
    inv = 1.0 / (ROPE_THETA ** (np.arange(0, AXIS_ROPE, 2, dtype=np.float64) / AXIS_ROPE))
    ang_r, ang_c = (pos // GRID_W)[:, None] * inv, (pos % GRID_W)[:, None] * inv
    cr, sr, cc, sc = np.cos(ang_r), np.sin(ang_r), np.cos(ang_c), np.sin(ang_c)
    zero = np.zeros_like(sr)
    cos = np.concatenate([cr, cr, cc, cc], axis=1)
    sin_a = np.concatenate([-sr, zero, -sc, zero], axis=1)
    sin_b = np.concatenate([zero, sr, zero, sc], axis=1)
    return tuple(jnp.asarray(np.tile(t, (1, LANES // QK_ROPE)), F32) for t in (cos, sin_a, sin_b))


def _attn_weights(w_in, gq, gkv, w_uq, w_ukv, w_o):
    n = w_in.shape[0]
    kpe0 = Q_RANK + KV_RANK
    w_in_t = jnp.swapaxes(w_in, 1, 2)
    w_in_p = jnp.concatenate(
        [w_in_t[:, :kpe0 + QK_ROPE], w_in_t[:, kpe0:kpe0 + QK_ROPE], w_in_t[:, kpe0 + QK_ROPE:]], axis=1)
    ukv = w_ukv.reshape(n, KV_RANK, N_HEADS, QK_NOPE + V_DIM)
    w_uk = ukv[..., :QK_NOPE].reshape(n, KV_RANK, -1)
    w_uvt = jnp.swapaxes(ukv[..., QK_NOPE:].reshape(n, KV_RANK, -1), 1, 2)
    return dict(w_in=w_in_p.astype(BF16), gq=gq[:, None, :], gkv=gkv[:, None, :],
                w_uq=w_uq.astype(BF16), w_uk=w_uk.astype(BF16), w_uvt=w_uvt.astype(BF16),
                w_o=w_o.astype(BF16))


def _mlp_weights(w_in_bf16, vg, vb, w_s, b_s, w_o_bf16):
    return dict(w_in=w_in_bf16, vg=vg[:, None, :], vb=vb[:, None, :],
                w_s=w_s.astype(BF16), b_s_t=jnp.swapaxes(b_s, 1, 2), w_o=w_o_bf16)


def kernel(x_prompt, x_sample, cache_ckv, cache_kpe, c, c_ctx, norm_g, w_mod, b_mod, attn_w_in, attn_q_norm_g, attn_kv_norm_g, attn_w_uq, attn_w_ukv, attn_w_o, mlp_w_in, mlp_v_norm_g, mlp_v_norm_b, mlp_w_s, mlp_b_s, mlp_w_o, final_norm_g):
    depth = norm_g.shape[0]
    bc, tc, _ = x_prompt.shape
    bl, tl, _ = x_sample.shape
    if depth % 2:
        raise NotImplementedError("final RMSNorm is fused into a trailing gMLP layer")

    cond8 = jnp.concatenate([c_ctx[None, :], c, jnp.zeros((8 - 1 - bl, D), F32)], axis=0)
    mods = _modulation(cond8, w_mod, b_mod).reshape(depth * 8, 1, 3 * D)
    rope = _rope_tables(tl)
    norm_g = norm_g[:, None, :]
    final_g = final_norm_g.reshape(1, D)
    aw = _attn_weights(attn_w_in, attn_q_norm_g, attn_kv_norm_g, attn_w_uq, attn_w_ukv, attn_w_o)
    kc, vtc = _cache_kv(cache_ckv, jnp.swapaxes(cache_kpe, 2, 3), aw["w_uk"], aw["w_uvt"])

    xc, xl = x_prompt, x_sample
    cache_out = ()
    for layer in range(depth):
        mod0 = 8 * layer
        if layer % 2 == 0:
            a = layer // 2
            if layer == 0:
                xc, *cache_out, w_in_b, w_o_b = _ctx_attn_layer(
                    xc, mods, mod0, layer, a, norm_g, aw, cache_out, riders=(mlp_w_in, mlp_w_o))
                mw = _mlp_weights(w_in_b, mlp_v_norm_g, mlp_v_norm_b, mlp_w_s, mlp_b_s, w_o_b)
            else:
                xc, *cache_out = _ctx_attn_layer(xc, mods, mod0, layer, a, norm_g, aw, cache_out)
            q, kl, vtl, z = _lat_proj(xl, mods, mod0 + 1, layer, a, norm_g, rope, aw)
            xl = _lat_attn(xl, mods, mod0 + 1, a, q, z, kc, vtc, kl, vtl, aw["w_o"])
        else:
            m = layer // 2
            last = layer == depth - 1
            xc = _mlp_layer(xc.reshape(1, bc * tc, D), mods, mod0, layer, m, norm_g, mw, final_g,
                            last).reshape(bc, tc, D)
            xl = _mlp_layer(xl, mods, mod0 + 1, layer, m, norm_g, mw, final_g, last)
    new_ckv, new_kpe_t = cache_out
    return xc, xl, new_ckv, jnp.swapaxes(new_kpe_t, 2, 3)
```
